```python
import jax, jax.numpy as jnp
from jax import lax
import numpy as np

D_MODEL = 1024
BATCH = 8
SEQ = 4096
DEPTH = 4

N_HEADS = 8
QK_NOPE = 64
QK_ROPE = 32
QK_HEAD = QK_NOPE + QK_ROPE
V_HEAD = 64
Q_LORA = 384
KV_LORA = 256
ATTN_WIDTH = N_HEADS * V_HEAD
CONV_WIDTH = D_MODEL - ATTN_WIDTH
CONV_TAPS = 3
IN_COLS = Q_LORA + KV_LORA + QK_ROPE + 3 * CONV_WIDTH
D_FF = 4 * D_MODEL
PLE_DIM = 256
ROPE_THETA = 10000.0
Q_BLOCK = 128
EPS = 1e-6
MAX_POS_OFFSET = 1024

kernel_name = "hybrid_mla_shortconv_trunk"


def rmsnorm(x, g):
    xf = x.astype(jnp.float32)
    y = xf * lax.rsqrt(jnp.mean(xf * xf, axis=-1, keepdims=True) + EPS)
    return (y * g.astype(jnp.float32)).astype(x.dtype)


def rope_tables(positions):
    inv_freq = 1.0 / (ROPE_THETA ** (jnp.arange(0, QK_ROPE, 2, dtype=jnp.float32) / QK_ROPE))
    ang = positions.astype(jnp.float32)[..., None] * inv_freq
    return jnp.cos(ang)[:, :, None, :], jnp.sin(ang)[:, :, None, :]


def apply_rope(x, cos, sin):
    half = QK_ROPE // 2
    x1 = x[..., :half].astype(jnp.float32)
    x2 = x[..., half:].astype(jnp.float32)
    return jnp.concatenate([x1 * cos - x2 * sin, x2 * cos + x1 * sin], axis=-1).astype(x.dtype)


def causal_block_attention(q, k, v):
    b, s = q.shape[0], q.shape[1]
    scale = QK_HEAD ** -0.5
    n_blocks = s // Q_BLOCK
    k_idx = jnp.arange(s)

    def one_block(i):
        start = i * Q_BLOCK
        qb = lax.dynamic_slice_in_dim(q, start, Q_BLOCK, axis=1)
        sc = jnp.einsum('bqhd,bkhd->bhqk', qb, k, preferred_element_type=jnp.float32) * scale
        q_idx = start + jnp.arange(Q_BLOCK)
        sc = jnp.where(k_idx[None, :] <= q_idx[:, None], sc, -jnp.inf)
        pr = jax.nn.softmax(sc, axis=-1).astype(v.dtype)
        return jnp.einsum('bhqk,bkhd->bqhd', pr, v)

    out = lax.map(one_block, jnp.arange(n_blocks))
    return jnp.moveaxis(out, 0, 1).reshape(b, s, N_HEADS * V_HEAD)


def mla_group(q_lat, kv_lat, k_pe, cos, sin, g_q_lat, w_uq, g_kv_lat, w_ukv,
              g_qn_nope, g_qn_rope, g_kn_nope, g_kn_rope):
    b, s = q_lat.shape[0], q_lat.shape[1]
    q = (rmsnorm(q_lat, g_q_lat) @ w_uq).reshape(b, s, N_HEADS, QK_HEAD)
    kv = (rmsnorm(kv_lat, g_kv_lat) @ w_ukv).reshape(b, s, N_HEADS, QK_NOPE + V_HEAD)
    k_nope, v = kv[..., :QK_NOPE], kv[..., QK_NOPE:]
    q_nope = rmsnorm(q[..., :QK_NOPE], g_qn_nope)
    q_pe = apply_rope(rmsnorm(q[..., QK_NOPE:], g_qn_rope), cos, sin)
    k_nope = rmsnorm(k_nope, g_kn_nope)
    k_pe = apply_rope(rmsnorm(k_pe.reshape(b, s, 1, QK_ROPE), g_kn_rope), cos, sin)
    qf = jnp.concatenate([q_nope, q_pe], axis=-1)
    kf = jnp.concatenate([k_nope, jnp.broadcast_to(k_pe, (b, s, N_HEADS, QK_ROPE))], axis=-1)
    return causal_block_attention(qf, kf, v)


def short_conv_group(gate_b, gate_c, x_in, conv_w):
    u = gate_c * x_in
    s = u.shape[1]
    up = jnp.pad(u, ((0, 0), (CONV_TAPS - 1, 0), (0, 0)))
    y = sum(conv_w[j] * up[:, CONV_TAPS - 1 - j: CONV_TAPS - 1 - j + s] for j in range(CONV_TAPS))
    return gate_b * y


def setup_inputs(seed: int = 0) -> dict:
    key = jax.random.key(seed)
    ks = jax.random.split(key, 24)

    def w(k, shape, fan_in):
        return jax.random.normal(k, (DEPTH,) + shape, jnp.float32) * fan_in ** -0.5

    def gain(k, n):
        return 1.0 + 0.02 * jax.random.normal(k, (DEPTH, n), jnp.float32)

    x = jax.random.normal(ks[0], (BATCH, SEQ, D_MODEL), jnp.float32)
    p = jax.random.normal(ks[1], (DEPTH, BATCH, SEQ, PLE_DIM), jnp.float32)
    offs = jax.random.randint(ks[2], (BATCH, 1), 0, MAX_POS_OFFSET, dtype=jnp.int32)
    positions = (offs + jnp.arange(SEQ, dtype=jnp.int32)[None, :]).astype(jnp.int32)
    return {
        "x": x,
        "p": p,
        "positions": positions,
        "g_mix": gain(ks[3], D_MODEL),
        "w_in": w(ks[4], (D_MODEL, IN_COLS), D_MODEL),
        "g_q_lat": gain(ks[5], Q_LORA),
        "w_uq": w(ks[6], (Q_LORA, N_HEADS * QK_HEAD), Q_LORA),
        "g_kv_lat": gain(ks[7], KV_LORA),
        "w_ukv": w(ks[8], (KV_LORA, N_HEADS * (QK_NOPE + V_HEAD)), KV_LORA),
        "g_qn_nope": gain(ks[9], QK_NOPE),
        "g_qn_rope": gain(ks[10], QK_ROPE),
        "g_kn_nope": gain(ks[11], QK_NOPE),
        "g_kn_rope": gain(ks[12], QK_ROPE),
        "conv_w": w(ks[13], (CONV_TAPS, CONV_WIDTH), CONV_TAPS),
        "g_out_attn": gain(ks[14], ATTN_WIDTH),
        "g_out_conv": gain(ks[15], CONV_WIDTH),
        "w_o": w(ks[16], (D_MODEL, D_MODEL), D_MODEL),
        "g_mlp": gain(ks[17], D_MODEL),
        "w_up": w(ks[18], (D_MODEL, D_FF), D_MODEL),
        "w_down": w(ks[19], (D_FF, D_MODEL), D_FF),
        "g_ple": gain(ks[20], D_MODEL),
        "w_ple_gate": w(ks[21], (D_MODEL, D_MODEL), D_MODEL),
        "w_ple": w(ks[22], (PLE_DIM, D_MODEL), PLE_DIM),
    }


def reference(x, p, positions, g_mix, w_in, g_q_lat, w_uq, g_kv_lat, w_ukv,
              g_qn_nope, g_qn_rope, g_kn_nope, g_kn_rope, conv_w, g_out_attn,
              g_out_conv, w_o, g_mlp, w_up, w_down, g_ple, w_ple_gate, w_ple):
    cos, sin = rope_tables(positions)
    o1 = Q_LORA
    o2 = o1 + KV_LORA
    o3 = o2 + QK_ROPE
    o4 = o3 + CONV_WIDTH
    o5 = o4 + CONV_WIDTH
    for i in range(DEPTH):
        h = rmsnorm(x, g_mix[i])
        z = h @ w_in[i]
        attn = mla_group(z[..., :o1], z[..., o1:o2], z[..., o2:o3], cos, sin,
                         g_q_lat[i], w_uq[i], g_kv_lat[i], w_ukv[i],
                         g_qn_nope[i], g_qn_rope[i], g_kn_nope[i], g_kn_rope[i])
        conv = short_conv_group(z[..., o3:o4], z[..., o4:o5], z[..., o5:], conv_w[i])
        mixed = jnp.concatenate([rmsnorm(attn, g_out_attn[i]), rmsnorm(conv, g_out_conv[i])], axis=-1)
        x = x + mixed @ w_o[i]
        h2 = rmsnorm(x, g_mlp[i])
        x = x + jnp.square(jax.nn.relu(h2 @ w_up[i])) @ w_down[i]
        gate = jax.nn.sigmoid(rmsnorm(x, g_ple[i]) @ w_ple_gate[i])
        x = x + gate * (p[i] @ w_ple[i])
    return x
```

```python
import functools

import jax
import jax.numpy as jnp
from jax import lax
from jax.experimental import pallas as pl
from jax.experimental.pallas import tpu as pltpu

D_MODEL = 1024
N_HEADS = 8
QK_NOPE = 64
QK_ROPE = 32
QK_HEAD = QK_NOPE + QK_ROPE
V_HEAD = 64
Q_LORA = 384
KV_LORA = 256
ATTN_WIDTH = N_HEADS * V_HEAD
CONV_WIDTH = D_MODEL - ATTN_WIDTH
CONV_TAPS = 3
D_FF = 4 * D_MODEL
PLE_DIM = 256
ROPE_THETA = 10000.0
EPS = 1e-6

LANES = 128
SUBLANES = 8
HEAD_PAD = LANES
QK_WIDTH = N_HEADS * HEAD_PAD
ROPE_LO = QK_NOPE
ROPE_MID = QK_NOPE + QK_ROPE // 2
ROPE_HI = QK_NOPE + QK_ROPE
C_Q = 0
C_KV = C_Q + Q_LORA
C_KPE = C_KV + KV_LORA
C_B = C_KPE + LANES
C_C = C_B + CONV_WIDTH
C_X = C_C + CONV_WIDTH
IN_COLS_PAD = C_X + CONV_WIDTH

ROW_TILE = 512
Q_TILE = 512
KV_TILE = 512
FF_CHUNK = 1024
VMEM_LIMIT = 56 * 1024 * 1024

_BF16 = jnp.bfloat16
_F32 = jnp.float32


def _dot(a, b):
    return jnp.dot(a, b, preferred_element_type=_F32)


def _rms(x, g, n):
    ms = jnp.sum(x * x, axis=-1, keepdims=True) * (1.0 / n)
    return x * lax.rsqrt(ms + EPS) * g


def _rope_partner(t, lane):
    return jnp.where(lane < ROPE_MID, pltpu.roll(t, LANES - QK_ROPE // 2, axis=1),
                     pltpu.roll(t, QK_ROPE // 2, axis=1))


def _rope_table_kernel(pos_ref, invf_ref, sign_ref, cos_ref, sin_ref):
    ang = pos_ref[0].astype(_F32) * invf_ref[...]
    cos_ref[0] = jnp.cos(ang)
    sin_ref[0] = jnp.sin(ang) * sign_ref[...]


def _rope_tables(positions):
    b, s = positions.shape
    half = QK_ROPE // 2
    inv_freq = 1.0 / (ROPE_THETA ** (jnp.arange(0, QK_ROPE, 2, dtype=_F32) / QK_ROPE))
    zeros = jnp.zeros((half,), _F32)
    invf = jnp.concatenate([jnp.zeros((ROPE_LO,), _F32), inv_freq, inv_freq,
                            jnp.zeros((LANES - ROPE_HI,), _F32)]).reshape(1, LANES)
    sign = jnp.concatenate([jnp.zeros((ROPE_LO,), _F32), zeros - 1.0, zeros + 1.0,
                            jnp.zeros((LANES - ROPE_HI,), _F32)]).reshape(1, LANES)
    t = ROW_TILE
    const = pl.BlockSpec((1, LANES), lambda i, j: (0, 0))
    out = pl.BlockSpec((1, t, LANES), lambda i, j: (i, j, 0))
    return pl.pallas_call(
        _rope_table_kernel,
        out_shape=(jax.ShapeDtypeStruct((b, s, LANES), _F32),) * 2,
        grid=(b, s // t),
        in_specs=[pl.BlockSpec((1, t, 1), lambda i, j: (i, j, 0)), const, const],
        out_specs=(out, out),
        name="rope_tables",
    )(positions.reshape(b, s, 1), invf, sign)


def _proj_kernel(x_ref, cos_ref, sin_ref, gmix_ref, win_ref, gq_ref, wuq_ref, gkv_ref,
                 wuk_ref, wuv_ref, gqn_ref, gkn_ref, gkpe_ref, convw_ref, gconv_ref,
                 q_out, k_out, v_out, conv_out, u_ref):
    t = x_ref.shape[1]
    lane = lax.broadcasted_iota(jnp.int32, (1, LANES), 1)
    cos = cos_ref[0]
    sin = sin_ref[0]

    h = _rms(x_ref[0], gmix_ref[...], D_MODEL).astype(_BF16)

    qn = _rms(_dot(h, win_ref[:, C_Q:C_KV]), gq_ref[...], Q_LORA).astype(_BF16)
    qf = _dot(qn, wuq_ref[...])
    scale = QK_HEAD ** -0.5
    for hd in range(N_HEADS):
        th = qf[:, hd * HEAD_PAD:(hd + 1) * HEAD_PAD]
        t2 = th * th
        ss_n = jnp.sum(jnp.where(lane < ROPE_LO, t2, 0.0), axis=-1, keepdims=True)
        ss_r = jnp.sum(jnp.where(lane >= ROPE_LO, t2, 0.0), axis=-1, keepdims=True)
        inv = jnp.where(lane < ROPE_LO, lax.rsqrt(ss_n * (1.0 / QK_NOPE) + EPS),
                        lax.rsqrt(ss_r * (1.0 / QK_ROPE) + EPS))
        tn = th * inv * gqn_ref[...]
        tr = tn * cos + _rope_partner(tn, lane) * sin
        q_out[0, :, hd * HEAD_PAD:(hd + 1) * HEAD_PAD] = (tr * scale).astype(_BF16)

    kvn = _rms(_dot(h, win_ref[:, C_KV:C_KPE]), gkv_ref[...], KV_LORA).astype(_BF16)
    v_out[0] = _dot(kvn, wuv_ref[...]).astype(_BF16)
    kpe = _dot(h, win_ref[:, C_KPE:C_B])
    ss_pe = jnp.sum(kpe * kpe, axis=-1, keepdims=True)
    kpe_n = kpe * lax.rsqrt(ss_pe * (1.0 / QK_ROPE) + EPS) * gkpe_ref[...]
    kpe_r = kpe_n * cos + _rope_partner(kpe_n, lane) * sin
    kf = _dot(kvn, wuk_ref[...])
    for hd in range(N_HEADS):
        th = kf[:, hd * HEAD_PAD:(hd + 1) * HEAD_PAD]
        ss = jnp.sum(th * th, axis=-1, keepdims=True)
        kh = th * lax.rsqrt(ss * (1.0 / QK_NOPE) + EPS) * gkn_ref[...] + kpe_r
        k_out[0, :, hd * HEAD_PAD:(hd + 1) * HEAD_PAD] = kh.astype(_BF16)

    @pl.when(pl.program_id(1) == 0)
    def _():
        u_ref[0:SUBLANES, :] = jnp.zeros((SUBLANES, CONV_WIDTH), _F32)

    gate_b = _dot(h, win_ref[:, C_B:C_C])
    u = _dot(h, win_ref[:, C_C:C_X]) * _dot(h, win_ref[:, C_X:IN_COLS_PAD])
    u_ref[SUBLANES:SUBLANES + t, :] = u
    y = convw_ref[0:1, :] * u
    for j in range(1, CONV_TAPS):
        y = y + convw_ref[j:j + 1, :] * u_ref[SUBLANES - j:SUBLANES - j + t, :]
    u_ref[0:SUBLANES, :] = u[t - SUBLANES:t, :]
    conv_out[0] = _rms(gate_b * y, gconv_ref[...], CONV_WIDTH).astype(_BF16)


def _proj_call(layer, x, cos, sin, w):
    b, s, _ = x.shape
    t = ROW_TILE

    def row(width):
        return pl.BlockSpec((1, t, width), lambda i, j: (i, j, 0))

    def per_layer(arr):
        return pl.BlockSpec((None,) + arr.shape[1:], lambda i, j: (layer,) + (0,) * (arr.ndim - 1))

    params = [w["g_mix"], w["w_in"], w["g_q_lat"], w["w_uq"], w["g_kv_lat"], w["w_uk"], w["w_uv"],
              w["g_qn"], w["g_kn"], w["g_kpe"], w["conv_w"], w["g_out_conv"]]
    return pl.pallas_call(
        _proj_kernel,
        out_shape=(jax.ShapeDtypeStruct((b, s, QK_WIDTH), _BF16),
                   jax.ShapeDtypeStruct((b, s, QK_WIDTH), _BF16),
                   jax.ShapeDtypeStruct((b, s, ATTN_WIDTH), _BF16),
                   jax.ShapeDtypeStruct((b, s, CONV_WIDTH), _BF16)),
        grid=(b, s // t),
        in_specs=[row(D_MODEL), row(LANES), row(LANES)] + [per_layer(a) for a in params],
        out_specs=(row(QK_WIDTH), row(QK_WIDTH), row(ATTN_WIDTH), row(CONV_WIDTH)),
        scratch_shapes=[pltpu.VMEM((SUBLANES + t, CONV_WIDTH), _F32)],
        compiler_params=pltpu.CompilerParams(
            dimension_semantics=("arbitrary", "arbitrary"), vmem_limit_bytes=VMEM_LIMIT),
        name="proj",
    )(x, cos, sin, *params)


def _attn_kernel(q_ref, k_ref, v_ref, o_ref, m_ref, l_ref, acc_ref):
    tq = q_ref.shape[1]
    tk = KV_TILE
    qi = pl.program_id(2)
    heads = q_ref.shape[2] // HEAD_PAD

    m_ref[...] = jnp.full(m_ref.shape, -jnp.inf, _F32)
    l_ref[...] = jnp.zeros(l_ref.shape, _F32)
    acc_ref[...] = jnp.zeros(acc_ref.shape, _F32)

    def step(j, masked):
        start = pl.multiple_of(j * tk, tk)
        vblk = v_ref[0, pl.ds(start, tk), :]
        for hh in range(heads):
            q = q_ref[0, :, hh * HEAD_PAD:(hh + 1) * HEAD_PAD]
            kblk = k_ref[0, pl.ds(start, tk), hh * HEAD_PAD:(hh + 1) * HEAD_PAD]
            s = lax.dot_general(q, kblk, (((1,), (1,)), ((), ())), preferred_element_type=_F32)
            if masked:
                row = lax.broadcasted_iota(jnp.int32, (tq, tk), 0)
                col = lax.broadcasted_iota(jnp.int32, (tq, tk), 1)
                s = jnp.where(col <= row, s, -jnp.inf)
            m_old = m_ref[hh]
            m_new = jnp.maximum(m_old, jnp.max(s, axis=-1, keepdims=True))
            alpha = jnp.exp(m_old - m_new)
            p = jnp.exp(s - m_new)
            l_ref[hh] = alpha * l_ref[hh] + jnp.sum(p, axis=-1, keepdims=True)
            acc_ref[hh] = alpha * acc_ref[hh] + _dot(p.astype(_BF16), vblk)
            m_ref[hh] = m_new

    def body(j, carry):
        step(j, False)
        return carry

    lax.fori_loop(0, qi * (tq // tk), body, 0)
    for d in range(tq // tk):
        step(qi * (tq // tk) + d, True)

    lane = lax.broadcasted_iota(jnp.int32, (1, LANES), 1)
    out = acc_ref[0] / l_ref[0]
    for hh in range(1, heads):
        out = jnp.where(lane < hh * V_HEAD, out, acc_ref[hh] / l_ref[hh])
    o_ref[0] = out.astype(o_ref.dtype)


def _attn_call(q, k, v):
    b, s, _ = q.shape
    heads = LANES // V_HEAD
    groups = N_HEADS // heads
    tq = Q_TILE
    assert Q_TILE == KV_TILE
    return pl.pallas_call(
        _attn_kernel,
        out_shape=jax.ShapeDtypeStruct((b, s, ATTN_WIDTH), _BF16),
        grid=(b, groups, s // tq),
        in_specs=[pl.BlockSpec((1, tq, heads * HEAD_PAD), lambda i, g, j: (i, j, g)),
                  pl.BlockSpec((1, s, heads * HEAD_PAD), lambda i, g, j: (i, 0, g)),
                  pl.BlockSpec((1, s, LANES), lambda i, g, j: (i, 0, g))],
        out_specs=pl.BlockSpec((1, tq, LANES), lambda i, g, j: (i, j, g)),
        scratch_shapes=[pltpu.VMEM((heads, tq, 1), _F32), pltpu.VMEM((heads, tq, 1), _F32),
                        pltpu.VMEM((heads, tq, LANES), _F32)],
        compiler_params=pltpu.CompilerParams(
            dimension_semantics=("arbitrary", "arbitrary", "arbitrary"),
            vmem_limit_bytes=VMEM_LIMIT),
        name="attn",
    )(q, k, v)


def _post_kernel(x_ref, attn_ref, conv_ref, p_ref, gattn_ref, wo_ref, gmlp_ref, wup_ref,
                 wdown_ref, gple_ref, wgate_ref, wple_ref, o_ref):
    an = _rms(attn_ref[...].astype(_F32), gattn_ref[...], ATTN_WIDTH).astype(_BF16)
    x = x_ref[...] + _dot(an, wo_ref[0:ATTN_WIDTH, :]) + _dot(conv_ref[...], wo_ref[ATTN_WIDTH:, :])

    h2 = _rms(x, gmlp_ref[...], D_MODEL).astype(_BF16)
    mlp = None
    for c in range(D_FF // FF_CHUNK):
        up = _dot(h2, wup_ref[:, c * FF_CHUNK:(c + 1) * FF_CHUNK])
        act = jnp.square(jnp.maximum(up, 0.0)).astype(_BF16)
        part = _dot(act, wdown_ref[c * FF_CHUNK:(c + 1) * FF_CHUNK, :])
        mlp = part if mlp is None else mlp + part
    x = x + mlp

    gate = jax.nn.sigmoid(_dot(_rms(x, gple_ref[...], D_MODEL).astype(_BF16), wgate_ref[...]))
    o_ref[...] = x + gate * _dot(p_ref[...].astype(_BF16), wple_ref[...])


def _post_call(layer, x, attn, conv, p, w):
    n = x.shape[0]
    t = ROW_TILE

    def row(width):
        return pl.BlockSpec((t, width), lambda i: (i, 0))

    def per_layer(arr):
        return pl.BlockSpec((None,) + arr.shape[1:], lambda i: (layer,) + (0,) * (arr.ndim - 1),
                            pipeline_mode=pl.Buffered(1))

    params = [w["g_out_attn"], w["w_o"], w["g_mlp"], w["w_up"], w["w_down"], w["g_ple"],
              w["w_ple_gate"], w["w_ple"]]
    return pl.pallas_call(
        _post_kernel,
        out_shape=jax.ShapeDtypeStruct((n, D_MODEL), _F32),
        grid=(n // t,),
        in_specs=[row(D_MODEL), row(ATTN_WIDTH), row(CONV_WIDTH),
                  pl.BlockSpec((None, t, PLE_DIM), lambda i: (layer, i, 0))]
                 + [per_layer(a) for a in params],
        out_specs=row(D_MODEL),
        compiler_params=pltpu.CompilerParams(
            dimension_semantics=("arbitrary",), vmem_limit_bytes=VMEM_LIMIT),
        name="post",
    )(x, attn, conv, p, *params)


def _prepare_weights(g_mix, w_in, g_q_lat, w_uq, g_kv_lat, w_ukv, g_qn_nope, g_qn_rope,
                     g_kn_nope, g_kn_rope, conv_w, g_out_attn, g_out_conv, w_o, g_mlp, w_up,
                     w_down, g_ple, w_ple_gate, w_ple):
    depth = w_in.shape[0]

    def gain(g):
        return g.reshape(depth, 1, -1).astype(_F32)

    def zeros(*shape):
        return jnp.zeros((depth,) + shape, _F32)

    o_kpe = Q_LORA + KV_LORA
    w_in_p = jnp.concatenate(
        [w_in[..., :o_kpe], zeros(D_MODEL, ROPE_LO), w_in[..., o_kpe:o_kpe + QK_ROPE],
         zeros(D_MODEL, LANES - ROPE_HI), w_in[..., o_kpe + QK_ROPE:]], axis=-1)
    w_uq_p = jnp.pad(w_uq.reshape(depth, Q_LORA, N_HEADS, QK_HEAD),
                     ((0, 0), (0, 0), (0, 0), (0, HEAD_PAD - QK_HEAD))).reshape(depth, Q_LORA, QK_WIDTH)
    w_ukv_h = w_ukv.reshape(depth, KV_LORA, N_HEADS, QK_NOPE + V_HEAD)
    w_uk_p = jnp.pad(w_ukv_h[..., :QK_NOPE],
                     ((0, 0), (0, 0), (0, 0), (0, HEAD_PAD - QK_NOPE))).reshape(depth, KV_LORA, QK_WIDTH)
    w_uv = w_ukv_h[..., QK_NOPE:].reshape(depth, KV_LORA, ATTN_WIDTH)
    return {
        "g_mix": gain(g_mix), "w_in": w_in_p.astype(_BF16),
        "g_q_lat": gain(g_q_lat), "w_uq": w_uq_p.astype(_BF16),
        "g_kv_lat": gain(g_kv_lat), "w_uk": w_uk_p.astype(_BF16), "w_uv": w_uv.astype(_BF16),
        "g_qn": gain(jnp.concatenate([g_qn_nope, g_qn_rope, zeros(LANES - ROPE_HI)], axis=-1)),
        "g_kn": gain(jnp.concatenate([g_kn_nope, zeros(LANES - QK_NOPE)], axis=-1)),
        "g_kpe": gain(jnp.concatenate([zeros(ROPE_LO), g_kn_rope, zeros(LANES - ROPE_HI)], axis=-1)),
        "conv_w": conv_w.astype(_F32), "g_out_conv": gain(g_out_conv),
        "g_out_attn": gain(g_out_attn), "w_o": w_o.astype(_BF16), "g_mlp": gain(g_mlp),
        "w_up": w_up.astype(_BF16), "w_down": w_down.astype(_BF16), "g_ple": gain(g_ple),
        "w_ple_gate": w_ple_gate.astype(_BF16), "w_ple": w_ple.astype(_BF16),
    }


def kernel(x, p, positions, g_mix, w_in, g_q_lat, w_uq, g_kv_lat, w_ukv, g_qn_nope, g_qn_rope,
           g_kn_nope, g_kn_rope, conv_w, g_out_attn, g_out_conv, w_o, g_mlp, w_up, w_down, g_ple,
           w_ple_gate, w_ple):
    b, s, d = x.shape
    depth = w_in.shape[0]
    assert d == D_MODEL and s % ROW_TILE == 0 and s % Q_TILE == 0
    w = _prepare_weights(g_mix, w_in, g_q_lat, w_uq, g_kv_lat, w_ukv, g_qn_nope, g_qn_rope,
                         g_kn_nope, g_kn_rope, conv_w, g_out_attn, g_out_conv, w_o, g_mlp, w_up,
                         w_down, g_ple, w_ple_gate, w_ple)
    cos, sin = _rope_tables(positions)
    p_rows = p.reshape(depth, b * s, PLE_DIM)
    for layer in range(depth):
        q, k, v, conv = _proj_call(layer, x, cos, sin, w)
        attn = _attn_call(q, k, v)
        x = _post_call(layer, x.reshape(b * s, d), attn.reshape(b * s, ATTN_WIDTH),
                       conv.reshape(b * s, CONV_WIDTH), p_rows, w).reshape(b, s, d)
    return x
```

```python
import functools

import jax
import jax.numpy as jnp
from jax import lax
from jax.experimental import pallas as pl
from jax.experimental.pallas import tpu as pltpu

D_MODEL = 1024
N_HEADS = 8
QK_NOPE = 64
QK_ROPE = 32
QK_HEAD = QK_NOPE + QK_ROPE
V_HEAD = 64
Q_LORA = 384
KV_LORA = 256
ATTN_WIDTH = N_HEADS * V_HEAD
CONV_WIDTH = D_MODEL - ATTN_WIDTH
CONV_TAPS = 3
D_FF = 4 * D_MODEL
PLE_DIM = 256
ROPE_THETA = 10000.0
EPS = 1e-6
LOG2_E = 1.4426950408889634

LANES = 128
SUBLANES = 8
HEAD_PAD = LANES
QK_WIDTH = N_HEADS * HEAD_PAD
ROPE_LO = QK_NOPE
ROPE_MID = QK_NOPE + QK_ROPE // 2
ROPE_HI = QK_NOPE + QK_ROPE
C_Q = 0
C_KV = C_Q + Q_LORA
C_KPE = C_KV + KV_LORA
C_B = C_KPE + LANES
C_C = C_B + CONV_WIDTH
C_X = C_C + CONV_WIDTH
IN_COLS_PAD = C_X + CONV_WIDTH

ROW_TILE = 512
Q_TILE = 512
KV_TILE = 512
ATTN_HEADS_PER_STEP = 4
FF_CHUNK = 1024
VMEM_LIMIT = 56 * 1024 * 1024

_BF16 = jnp.bfloat16
_F32 = jnp.float32


def _dot(a, b):
    return jnp.dot(a, b, preferred_element_type=_F32)


def _rms(x, g, n):
    ms = jnp.sum(x * x, axis=-1, keepdims=True) * (1.0 / n)
    return x * lax.rsqrt(ms + EPS) * g


def _rope_partner(t, lane):
    return jnp.where(lane < ROPE_MID, pltpu.roll(t, LANES - QK_ROPE // 2, axis=1),
                     pltpu.roll(t, QK_ROPE // 2, axis=1))


def _rope_table_kernel(pos_ref, invf_ref, sign_ref, cos_ref, sin_ref):
    ang = pos_ref[0].astype(_F32) * invf_ref[...]
    cos_ref[0] = jnp.cos(ang)
    sin_ref[0] = jnp.sin(ang) * sign_ref[...]


def _rope_tables(positions):
    b, s = positions.shape
    half = QK_ROPE // 2
    inv_freq = 1.0 / (ROPE_THETA ** (jnp.arange(0, QK_ROPE, 2, dtype=_F32) / QK_ROPE))
    zeros = jnp.zeros((half,), _F32)
    invf = jnp.concatenate([jnp.zeros((ROPE_LO,), _F32), inv_freq, inv_freq,
                            jnp.zeros((LANES - ROPE_HI,), _F32)]).reshape(1, LANES)
    sign = jnp.concatenate([jnp.zeros((ROPE_LO,), _F32), zeros - 1.0, zeros + 1.0,
                            jnp.zeros((LANES - ROPE_HI,), _F32)]).reshape(1, LANES)
    t = ROW_TILE
    const = pl.BlockSpec((1, LANES), lambda i, j: (0, 0))
    out = pl.BlockSpec((1, t, LANES), lambda i, j: (i, j, 0))
    return pl.pallas_call(
        _rope_table_kernel,
        out_shape=(jax.ShapeDtypeStruct((b, s, LANES), _F32),) * 2,
        grid=(b, s // t),
        in_specs=[pl.BlockSpec((1, t, 1), lambda i, j: (i, j, 0)), const, const],
        out_specs=(out, out),
        name="rope_tables",
    )(positions.reshape(b, s, 1), invf, sign)


def _proj_kernel(x_ref, cos_ref, sin_ref, gmix_ref, win_ref, gq_ref, wuq_ref, gkv_ref,
                 wuk_ref, wuvt_ref, gqn_ref, gkn_ref, gkpe_ref, convw_ref, gconv_ref,
                 q_out, k_out, vt_out, conv_out, u_ref):
    t = x_ref.shape[1]
    lane = lax.broadcasted_iota(jnp.int32, (1, LANES), 1)
    cos = cos_ref[0]
    sin = sin_ref[0]

    h = _rms(x_ref[0], gmix_ref[...], D_MODEL).astype(_BF16)

    qn = _rms(_dot(h, win_ref[:, C_Q:C_KV]), gq_ref[...], Q_LORA).astype(_BF16)
    qf = _dot(qn, wuq_ref[...])
    scale = QK_HEAD ** -0.5 * LOG2_E
    for hd in range(N_HEADS):
        th = qf[:, hd * HEAD_PAD:(hd + 1) * HEAD_PAD]
        t2 = th * th
        ss_n = jnp.sum(jnp.where(lane < ROPE_LO, t2, 0.0), axis=-1, keepdims=True)
        ss_r = jnp.sum(jnp.where(lane >= ROPE_LO, t2, 0.0), axis=-1, keepdims=True)
        inv = jnp.where(lane < ROPE_LO, lax.rsqrt(ss_n * (1.0 / QK_NOPE) + EPS),
                        lax.rsqrt(ss_r * (1.0 / QK_ROPE) + EPS))
        tn = th * inv * gqn_ref[...]
        tr = tn * cos + _rope_partner(tn, lane) * sin
        q_out[0, :, hd * HEAD_PAD:(hd + 1) * HEAD_PAD] = (tr * scale).astype(_BF16)

    kvn = _rms(_dot(h, win_ref[:, C_KV:C_KPE]), gkv_ref[...], KV_LORA).astype(_BF16)
    vt_out[0, 0] = lax.dot_general(wuvt_ref[...], kvn, (((1,), (1,)), ((), ())),
                                   preferred_element_type=_F32).astype(_BF16)
    kpe = _dot(h, win_ref[:, C_KPE:C_B])
    ss_pe = jnp.sum(kpe * kpe, axis=-1, keepdims=True)
    kpe_n = kpe * lax.rsqrt(ss_pe * (1.0 / QK_ROPE) + EPS) * gkpe_ref[...]
    kpe_r = kpe_n * cos + _rope_partner(kpe_n, lane) * sin
    kf = _dot(kvn, wuk_ref[...])
    for hd in range(N_HEADS):
        th = kf[:, hd * HEAD_PAD:(hd + 1) * HEAD_PAD]
        ss = jnp.sum(th * th, axis=-1, keepdims=True)
        kh = th * lax.rsqrt(ss * (1.0 / QK_NOPE) + EPS) * gkn_ref[...] + kpe_r
        k_out[0, :, hd * HEAD_PAD:(hd + 1) * HEAD_PAD] = kh.astype(_BF16)

    @pl.when(pl.program_id(1) == 0)
    def _():
        u_ref[0:SUBLANES, :] = jnp.zeros((SUBLANES, CONV_WIDTH), _F32)

    gate_b = _dot(h, win_ref[:, C_B:C_C])
    u = _dot(h, win_ref[:, C_C:C_X]) * _dot(h, win_ref[:, C_X:IN_COLS_PAD])
    u_ref[SUBLANES:SUBLANES + t, :] = u
    y = convw_ref[0:1, :] * u
    for j in range(1, CONV_TAPS):
        y = y + convw_ref[j:j + 1, :] * u_ref[SUBLANES - j:SUBLANES - j + t, :]
    u_ref[0:SUBLANES, :] = u[t - SUBLANES:t, :]
    conv_out[0] = _rms(gate_b * y, gconv_ref[...], CONV_WIDTH).astype(_BF16)


def _proj_call(layer, x, cos, sin, w):
    b, s, _ = x.shape
    t = ROW_TILE

    def row(width):
        return pl.BlockSpec((1, t, width), lambda i, j: (i, j, 0))

    def per_layer(arr):
        return pl.BlockSpec((None,) + arr.shape[1:], lambda i, j: (layer,) + (0,) * (arr.ndim - 1))

    params = [w["g_mix"], w["w_in"], w["g_q_lat"], w["w_uq"], w["g_kv_lat"], w["w_uk"], w["w_uvt"],
              w["g_qn"], w["g_kn"], w["g_kpe"], w["conv_w"], w["g_out_conv"]]
    assert t == KV_TILE
    return pl.pallas_call(
        _proj_kernel,
        out_shape=(jax.ShapeDtypeStruct((b, s, QK_WIDTH), _BF16),
                   jax.ShapeDtypeStruct((b, s, QK_WIDTH), _BF16),
                   jax.ShapeDtypeStruct((b, s // t, ATTN_WIDTH, t), _BF16),
                   jax.ShapeDtypeStruct((b, s, CONV_WIDTH), _BF16)),
        grid=(b, s // t),
        in_specs=[row(D_MODEL), row(LANES), row(LANES)] + [per_layer(a) for a in params],
        out_specs=(row(QK_WIDTH), row(QK_WIDTH),
                   pl.BlockSpec((1, 1, ATTN_WIDTH, t), lambda i, j: (i, j, 0, 0)),
                   row(CONV_WIDTH)),
        scratch_shapes=[pltpu.VMEM((SUBLANES + t, CONV_WIDTH), _F32)],
        compiler_params=pltpu.CompilerParams(
            dimension_semantics=("arbitrary", "arbitrary"), vmem_limit_bytes=VMEM_LIMIT),
        name="proj",
    )(x, cos, sin, *params)


def _attn_kernel(q_ref, k_ref, vt_ref, o_ref, m_ref, acc_ref):
    tq = q_ref.shape[1]
    tk = KV_TILE
    qi = pl.program_id(2)
    heads = q_ref.shape[2] // HEAD_PAD

    m_ref[...] = jnp.full(m_ref.shape, -jnp.inf, _F32)
    acc_ref[...] = jnp.zeros(acc_ref.shape, _F32)
    ones = jnp.ones((V_HEAD, tk), _BF16)

    def step(j, masked):
        start = pl.multiple_of(j * tk, tk)
        vts = [vt_ref[0, j, hh * V_HEAD:(hh + 1) * V_HEAD, :] for hh in range(heads)]
        sts = []
        for hh in range(heads):
            q = q_ref[0, :, hh * HEAD_PAD:(hh + 1) * HEAD_PAD]
            kblk = k_ref[0, pl.ds(start, tk), hh * HEAD_PAD:(hh + 1) * HEAD_PAD]
            sts.append(lax.dot_general(kblk, q, (((1,), (1,)), ((), ())),
                                       preferred_element_type=_F32))
        for hh in range(heads):
            st = sts[hh]
            if masked:
                key = lax.broadcasted_iota(jnp.int32, (tk, tq), 0)
                qry = lax.broadcasted_iota(jnp.int32, (tk, tq), 1)
                st = jnp.where(key <= qry, st, -jnp.inf)
            m_old = m_ref[hh]
            m_new = jnp.maximum(m_old, jnp.max(st, axis=0, keepdims=True))
            alpha = jnp.exp2(m_old - m_new)
            pt = jnp.exp2(st - m_new).astype(_BF16)
            vt = jnp.concatenate([vts[hh], ones], axis=0)
            acc_ref[hh] = alpha * acc_ref[hh] + _dot(vt, pt)
            m_ref[hh] = m_new

    def body(j, carry):
        step(j, False)
        return carry

    lax.fori_loop(0, qi, body, 0)
    step(qi, True)

    outs = []
    for hh in range(heads):
        acc = acc_ref[hh]
        outs.append(acc[0:V_HEAD, :] / acc[V_HEAD:V_HEAD + 1, :])
    o_ref[0] = jnp.concatenate(outs, axis=0).T.astype(o_ref.dtype)


def _attn_call(q, k, vt):
    b, s, _ = q.shape
    heads = ATTN_HEADS_PER_STEP
    assert (heads * V_HEAD) % LANES == 0 and N_HEADS % heads == 0
    groups = N_HEADS // heads
    tq = Q_TILE
    assert Q_TILE == KV_TILE
    return pl.pallas_call(
        _attn_kernel,
        out_shape=jax.ShapeDtypeStruct((b, s, ATTN_WIDTH), _BF16),
        grid=(b, groups, s // tq),
        in_specs=[pl.BlockSpec((1, tq, heads * HEAD_PAD), lambda i, g, j: (i, j, g)),
                  pl.BlockSpec((1, s, heads * HEAD_PAD), lambda i, g, j: (i, 0, g)),
                  pl.BlockSpec((1, s // KV_TILE, heads * V_HEAD, KV_TILE),
                               lambda i, g, j: (i, 0, g, 0))],
        out_specs=pl.BlockSpec((1, tq, heads * V_HEAD), lambda i, g, j: (i, j, g)),
        scratch_shapes=[pltpu.VMEM((heads, 1, tq), _F32),
                        pltpu.VMEM((heads, 2 * V_HEAD, tq), _F32)],
        compiler_params=pltpu.CompilerParams(
            dimension_semantics=("arbitrary", "arbitrary", "arbitrary"),
            vmem_limit_bytes=VMEM_LIMIT),
        name="attn",
    )(q, k, vt)


def _post_kernel(x_ref, attn_ref, conv_ref, p_ref, gattn_ref, wo_ref, gmlp_ref, wup_ref,
                 wdown_ref, gple_ref, wgate_ref, wple_ref, o_ref):
    an = _rms(attn_ref[...].astype(_F32), gattn_ref[...], ATTN_WIDTH).astype(_BF16)
    x = x_ref[...] + _dot(an, wo_ref[0:ATTN_WIDTH, :]) + _dot(conv_ref[...], wo_ref[ATTN_WIDTH:, :])

    h2 = _rms(x, gmlp_ref[...], D_MODEL).astype(_BF16)
    mlp = None
    for c in range(D_FF // FF_CHUNK):
        up = _dot(h2, wup_ref[:, c * FF_CHUNK:(c + 1) * FF_CHUNK])
        act = jnp.square(jnp.maximum(up, 0.0)).astype(_BF16)
        part = _dot(act, wdown_ref[c * FF_CHUNK:(c + 1) * FF_CHUNK, :])
        mlp = part if mlp is None else mlp + part
    x = x + mlp

    gate = jax.nn.sigmoid(_dot(_rms(x, gple_ref[...], D_MODEL).astype(_BF16), wgate_ref[...]))
    o_ref[...] = x + gate * _dot(p_ref[...].astype(_BF16), wple_ref[...])


def _post_call(layer, x, attn, conv, p, w):
    n = x.shape[0]
    t = ROW_TILE

    def row(width):
        return pl.BlockSpec((t, width), lambda i: (i, 0))

    def per_layer(arr):
        return pl.BlockSpec((None,) + arr.shape[1:], lambda i: (layer,) + (0,) * (arr.ndim - 1),
                            pipeline_mode=pl.Buffered(1))

    params = [w["g_out_attn"], w["w_o"], w["g_mlp"], w["w_up"], w["w_down"], w["g_ple"],
              w["w_ple_gate"], w["w_ple"]]
    return pl.pallas_call(
        _post_kernel,
        out_shape=jax.ShapeDtypeStruct((n, D_MODEL), _F32),
        grid=(n // t,),
        in_specs=[row(D_MODEL), row(ATTN_WIDTH), row(CONV_WIDTH),
                  pl.BlockSpec((None, t, PLE_DIM), lambda i: (layer, i, 0))]
                 + [per_layer(a) for a in params],
        out_specs=row(D_MODEL),
        compiler_params=pltpu.CompilerParams(
            dimension_semantics=("arbitrary",), vmem_limit_bytes=VMEM_LIMIT),
        name="post",
    )(x, attn, conv, p, *params)


def _prepare_weights(g_mix, w_in, g_q_lat, w_uq, g_kv_lat, w_ukv, g_qn_nope, g_qn_rope,
                     g_kn_nope, g_kn_rope, conv_w, g_out_attn, g_out_conv, w_o, g_mlp, w_up,
                     w_down, g_ple, w_ple_gate, w_ple):
    depth = w_in.shape[0]

    def gain(g):
        return g.reshape(depth, 1, -1).astype(_F32)

    def zeros(*shape):
        return jnp.zeros((depth,) + shape, _F32)

    o_kpe = Q_LORA + KV_LORA
    w_in_p = jnp.concatenate(
        [w_in[..., :o_kpe], zeros(D_MODEL, ROPE_LO), w_in[..., o_kpe:o_kpe + QK_ROPE],
         zeros(D_MODEL, LANES - ROPE_HI), w_in[..., o_kpe + QK_ROPE:]], axis=-1)
    w_uq_p = jnp.pad(w_uq.reshape(depth, Q_LORA, N_HEADS, QK_HEAD),
                     ((0, 0), (0, 0), (0, 0), (0, HEAD_PAD - QK_HEAD))).reshape(depth, Q_LORA, QK_WIDTH)
    w_ukv_h = w_ukv.reshape(depth, KV_LORA, N_HEADS, QK_NOPE + V_HEAD)
    w_uk_p = jnp.pad(w_ukv_h[..., :QK_NOPE],
                     ((0, 0), (0, 0), (0, 0), (0, HEAD_PAD - QK_NOPE))).reshape(depth, KV_LORA, QK_WIDTH)
    w_uvt = jnp.swapaxes(w_ukv_h[..., QK_NOPE:].reshape(depth, KV_LORA, ATTN_WIDTH), 1, 2)
    return {
        "g_mix": gain(g_mix), "w_in": w_in_p.astype(_BF16),
        "g_q_lat": gain(g_q_lat), "w_uq": w_uq_p.astype(_BF16),
        "g_kv_lat": gain(g_kv_lat), "w_uk": w_uk_p.astype(_BF16), "w_uvt": w_uvt.astype(_BF16),
        "g_qn": gain(jnp.concatenate([g_qn_nope, g_qn_rope, zeros(LANES - ROPE_HI)], axis=-1)),
        "g_kn": gain(jnp.concatenate([g_kn_nope, zeros(LANES - QK_NOPE)], axis=-1)),
        "g_kpe": gain(jnp.concatenate([zeros(ROPE_LO), g_kn_rope, zeros(LANES - ROPE_HI)], axis=-1)),
        "conv_w": conv_w.astype(_F32), "g_out_conv": gain(g_out_conv),
        "g_out_attn": gain(g_out_attn), "w_o": w_o.astype(_BF16), "g_mlp": gain(g_mlp),
        "w_up": w_up.astype(_BF16), "w_down": w_down.astype(_BF16), "g_ple": gain(g_ple),
        "w_ple_gate": w_ple_gate.astype(_BF16), "w_ple": w_ple.astype(_BF16),
    }


def kernel(x, p, positions, g_mix, w_in, g_q_lat, w_uq, g_kv_lat, w_ukv, g_qn_nope, g_qn_rope,
           g_kn_nope, g_kn_rope, conv_w, g_out_attn, g_out_conv, w_o, g_mlp, w_up, w_down, g_ple,
           w_ple_gate, w_ple):
    b, s, d = x.shape
    depth = w_in.shape[0]
    assert d == D_MODEL and s % ROW_TILE == 0 and s % Q_TILE == 0
    w = _prepare_weights(g_mix, w_in, g_q_lat, w_uq, g_kv_lat, w_ukv, g_qn_nope, g_qn_rope,
                         g_kn_nope, g_kn_rope, conv_w, g_out_attn, g_out_conv, w_o, g_mlp, w_up,
                         w_down, g_ple, w_ple_gate, w_ple)
    cos, sin = _rope_tables(positions)
    p_rows = p.reshape(depth, b * s, PLE_DIM)
    for layer in range(depth):
        q, k, vt, conv = _proj_call(layer, x, cos, sin, w)
        attn = _attn_call(q, k, vt)
        x = _post_call(layer, x.reshape(b * s, d), attn.reshape(b * s, ATTN_WIDTH),
                       conv.reshape(b * s, CONV_WIDTH), p_rows, w).reshape(b, s, d)
    return x
```

```python
import functools

import jax
import jax.numpy as jnp
from jax import lax
from jax.experimental import pallas as pl
from jax.experimental.pallas import tpu as pltpu

D_MODEL = 1024
N_HEADS = 8
QK_NOPE = 64
QK_ROPE = 32
QK_HEAD = QK_NOPE + QK_ROPE
V_HEAD = 64
Q_LORA = 384
KV_LORA = 256
ATTN_WIDTH = N_HEADS * V_HEAD
CONV_WIDTH = D_MODEL - ATTN_WIDTH
CONV_TAPS = 3
D_FF = 4 * D_MODEL
PLE_DIM = 256
ROPE_THETA = 10000.0
EPS = 1e-6
LOG2_E = 1.4426950408889634

LANES = 128
SUBLANES = 8
HEAD_PAD = LANES
QK_WIDTH = N_HEADS * HEAD_PAD
ROPE_LO = QK_NOPE
ROPE_MID = QK_NOPE + QK_ROPE // 2
ROPE_HI = QK_NOPE + QK_ROPE
C_Q = 0
C_KV = C_Q + Q_LORA
C_KPE = C_KV + KV_LORA
C_B = C_KPE + LANES
C_C = C_B + CONV_WIDTH
C_X = C_C + CONV_WIDTH
IN_COLS_PAD = C_X + CONV_WIDTH

ROW_TILE = 512
Q_TILE = 512
KV_TILE = 512
ATTN_HEADS_PER_STEP = 4
FF_CHUNK = 1024
VMEM_LIMIT = 56 * 1024 * 1024

_BF16 = jnp.bfloat16
_F32 = jnp.float32


def _dot(a, b):
    return jnp.dot(a, b, preferred_element_type=_F32)


def _rms(x, g, n):
    ms = jnp.sum(x * x, axis=-1, keepdims=True) * (1.0 / n)
    return x * lax.rsqrt(ms + EPS) * g


def _rope_partner(t, lane):
    return jnp.where(lane < ROPE_MID, pltpu.roll(t, LANES - QK_ROPE // 2, axis=1),
                     pltpu.roll(t, QK_ROPE // 2, axis=1))


def _rope_table_kernel(pos_col_ref, pos_row_ref, invf_ref, sign_ref, invf_t_ref,
                       cos_ref, sin_ref, cos_t_ref, sin_t_ref):
    ang = pos_col_ref[0].astype(_F32) * invf_ref[...]
    cos_ref[0] = jnp.cos(ang)
    sin_ref[0] = jnp.sin(ang) * sign_ref[...]
    ang_t = pos_row_ref[0].astype(_F32) * invf_t_ref[...]
    cos_t_ref[0] = jnp.cos(ang_t)
    sin_t_ref[0] = jnp.sin(ang_t)


def _rope_tables(positions):
    b, s = positions.shape
    half = QK_ROPE // 2
    inv_freq = 1.0 / (ROPE_THETA ** (jnp.arange(0, QK_ROPE, 2, dtype=_F32) / QK_ROPE))
    zeros = jnp.zeros((half,), _F32)
    invf = jnp.concatenate([jnp.zeros((ROPE_LO,), _F32), inv_freq, inv_freq,
                            jnp.zeros((LANES - ROPE_HI,), _F32)]).reshape(1, LANES)
    sign = jnp.concatenate([jnp.zeros((ROPE_LO,), _F32), zeros - 1.0, zeros + 1.0,
                            jnp.zeros((LANES - ROPE_HI,), _F32)]).reshape(1, LANES)
    t = ROW_TILE
    invf_t = jnp.broadcast_to(inv_freq.reshape(half, 1), (half, t))
    const = pl.BlockSpec((1, LANES), lambda i, j: (0, 0))
    out = pl.BlockSpec((1, t, LANES), lambda i, j: (i, j, 0))
    out_t = pl.BlockSpec((1, half, t), lambda i, j: (i, 0, j))
    return pl.pallas_call(
        _rope_table_kernel,
        out_shape=(jax.ShapeDtypeStruct((b, s, LANES), _F32),) * 2
        + (jax.ShapeDtypeStruct((b, half, s), _F32),) * 2,
        grid=(b, s // t),
        in_specs=[pl.BlockSpec((1, t, 1), lambda i, j: (i, j, 0)),
                  pl.BlockSpec((1, 1, t), lambda i, j: (i, 0, j)), const, const,
                  pl.BlockSpec((half, t), lambda i, j: (0, 0))],
        out_specs=(out, out, out_t, out_t),
        name="rope_tables",
    )(positions.reshape(b, s, 1), positions.reshape(b, 1, s), invf, sign, invf_t)


def _proj_kernel(x_ref, cos_ref, sin_ref, cos_t_ref, sin_t_ref, gmix_ref, win_ref, gq_ref,
                 wuqt_ref, gkv_ref, wuk_ref, wuvt_ref, gqn_t_ref, gkn_ref, gkpe_ref, convw_ref,
                 gconv_ref, qt_out, k_out, vt_out, conv_out, u_ref):
    t = x_ref.shape[1]
    lane = lax.broadcasted_iota(jnp.int32, (1, LANES), 1)
    cos = cos_ref[0]
    sin = sin_ref[0]
    nt = (((1,), (1,)), ((), ()))

    h = _rms(x_ref[0], gmix_ref[...], D_MODEL).astype(_BF16)

    qn = _rms(_dot(h, win_ref[:, C_Q:C_KV]), gq_ref[...], Q_LORA).astype(_BF16)
    qt = lax.dot_general(wuqt_ref[...], qn, nt, preferred_element_type=_F32)
    cos_t = cos_t_ref[0]
    sin_t = sin_t_ref[0]
    half = QK_ROPE // 2
    for hd in range(N_HEADS):
        r0 = hd * HEAD_PAD
        nope = qt[r0:r0 + ROPE_LO, :]
        rope = qt[r0 + ROPE_LO:r0 + ROPE_HI, :]
        ss_n = jnp.sum(nope * nope, axis=0, keepdims=True)
        ss_r = jnp.sum(rope * rope, axis=0, keepdims=True)
        nope = nope * lax.rsqrt(ss_n * (1.0 / QK_NOPE) + EPS) * gqn_t_ref[0:ROPE_LO, :]
        rope = rope * lax.rsqrt(ss_r * (1.0 / QK_ROPE) + EPS) * gqn_t_ref[ROPE_LO:ROPE_HI, :]
        x1 = rope[0:half, :]
        x2 = rope[half:QK_ROPE, :]
        qt_out[0, r0:r0 + ROPE_LO, :] = nope.astype(_BF16)
        qt_out[0, r0 + ROPE_LO:r0 + ROPE_MID, :] = (x1 * cos_t - x2 * sin_t).astype(_BF16)
        qt_out[0, r0 + ROPE_MID:r0 + ROPE_HI, :] = (x2 * cos_t + x1 * sin_t).astype(_BF16)
        qt_out[0, r0 + ROPE_HI:r0 + HEAD_PAD, :] = jnp.zeros((HEAD_PAD - ROPE_HI, t), _BF16)

    kvn = _rms(_dot(h, win_ref[:, C_KV:C_KPE]), gkv_ref[...], KV_LORA).astype(_BF16)
    vt_out[0, 0] = lax.dot_general(wuvt_ref[...], kvn, nt,
                                   preferred_element_type=_F32).astype(_BF16)
    kpe = _dot(h, win_ref[:, C_KPE:C_B])
    ss_pe = jnp.sum(kpe * kpe, axis=-1, keepdims=True)
    kpe_n = kpe * lax.rsqrt(ss_pe * (1.0 / QK_ROPE) + EPS) * gkpe_ref[...]
    kpe_r = kpe_n * cos + _rope_partner(kpe_n, lane) * sin
    kf = _dot(kvn, wuk_ref[...])
    for hd in range(N_HEADS):
        th = kf[:, hd * HEAD_PAD:(hd + 1) * HEAD_PAD]
        ss = jnp.sum(th * th, axis=-1, keepdims=True)
        kh = th * lax.rsqrt(ss * (1.0 / QK_NOPE) + EPS) * gkn_ref[...] + kpe_r
        k_out[0, :, hd * HEAD_PAD:(hd + 1) * HEAD_PAD] = kh.astype(_BF16)

    @pl.when(pl.program_id(1) == 0)
    def _():
        u_ref[0:SUBLANES, :] = jnp.zeros((SUBLANES, CONV_WIDTH), _F32)

    gate_b = _dot(h, win_ref[:, C_B:C_C])
    u = _dot(h, win_ref[:, C_C:C_X]) * _dot(h, win_ref[:, C_X:IN_COLS_PAD])
    u_ref[SUBLANES:SUBLANES + t, :] = u
    y = convw_ref[0:1, :] * u
    for j in range(1, CONV_TAPS):
        y = y + convw_ref[j:j + 1, :] * u_ref[SUBLANES - j:SUBLANES - j + t, :]
    u_ref[0:SUBLANES, :] = u[t - SUBLANES:t, :]
    conv_out[0] = _rms(gate_b * y, gconv_ref[...], CONV_WIDTH).astype(_BF16)


def _proj_call(layer, x, tables, w):
    b, s, _ = x.shape
    t = ROW_TILE
    cos, sin, cos_t, sin_t = tables

    def row(width):
        return pl.BlockSpec((1, t, width), lambda i, j: (i, j, 0))

    def col(height):
        return pl.BlockSpec((1, height, t), lambda i, j: (i, 0, j))

    def per_layer(arr):
        return pl.BlockSpec((None,) + arr.shape[1:], lambda i, j: (layer,) + (0,) * (arr.ndim - 1))

    params = [w["g_mix"], w["w_in"], w["g_q_lat"], w["w_uqt"], w["g_kv_lat"], w["w_uk"],
              w["w_uvt"], w["g_qn_t"], w["g_kn"], w["g_kpe"], w["conv_w"], w["g_out_conv"]]
    assert t == KV_TILE
    return pl.pallas_call(
        _proj_kernel,
        out_shape=(jax.ShapeDtypeStruct((b, QK_WIDTH, s), _BF16),
                   jax.ShapeDtypeStruct((b, s, QK_WIDTH), _BF16),
                   jax.ShapeDtypeStruct((b, s // t, ATTN_WIDTH, t), _BF16),
                   jax.ShapeDtypeStruct((b, s, CONV_WIDTH), _BF16)),
        grid=(b, s // t),
        in_specs=[row(D_MODEL), row(LANES), row(LANES), col(QK_ROPE // 2), col(QK_ROPE // 2)]
                 + [per_layer(a) for a in params],
        out_specs=(col(QK_WIDTH), row(QK_WIDTH),
                   pl.BlockSpec((1, 1, ATTN_WIDTH, t), lambda i, j: (i, j, 0, 0)),
                   row(CONV_WIDTH)),
        scratch_shapes=[pltpu.VMEM((SUBLANES + t, CONV_WIDTH), _F32)],
        compiler_params=pltpu.CompilerParams(
            dimension_semantics=("arbitrary", "arbitrary"), vmem_limit_bytes=VMEM_LIMIT),
        name="proj",
    )(x, cos, sin, cos_t, sin_t, *params)


def _attn_kernel(qt_ref, k_ref, vt_ref, o_ref, m_ref, acc_ref, s0_ref, s1_ref):
    tq = qt_ref.shape[2]
    tk = KV_TILE
    qi = pl.program_id(2)
    heads = qt_ref.shape[1] // HEAD_PAD

    m_ref[...] = jnp.full(m_ref.shape, -jnp.inf, _F32)
    acc_ref[...] = jnp.zeros(acc_ref.shape, _F32)
    ones = jnp.ones((V_HEAD, tk), _BF16)

    def load_values(t):
        return [vt_ref[0, t, hh * V_HEAD:(hh + 1) * V_HEAD, :] for hh in range(heads)]

    def score_stage(t, s_ref):
        start = pl.multiple_of(t * tk, tk)
        for hh in range(heads):
            qt = qt_ref[0, hh * HEAD_PAD:(hh + 1) * HEAD_PAD, :]
            kblk = k_ref[0, pl.ds(start, tk), hh * HEAD_PAD:(hh + 1) * HEAD_PAD]
            s_ref[hh] = _dot(kblk, qt)

    def softmax_stage(s_ref, vts, masked):
        for hh in range(heads):
            st = s_ref[hh]
            if masked:
                key = lax.broadcasted_iota(jnp.int32, (tk, tq), 0)
                qry = lax.broadcasted_iota(jnp.int32, (tk, tq), 1)
                st = jnp.where(key <= qry, st, -jnp.inf)
            m_old = m_ref[hh]
            m_new = jnp.maximum(m_old, jnp.max(st, axis=0, keepdims=True))
            alpha = jnp.exp2(m_old - m_new)
            pt = jnp.exp2(st - m_new).astype(_BF16)
            vt = jnp.concatenate([vts[hh], ones], axis=0)
            acc_ref[hh] = alpha * acc_ref[hh] + _dot(vt, pt)
            m_ref[hh] = m_new

    def pipelined_step(t, s_cur, s_next):
        vts = load_values(t)
        score_stage(t + 1, s_next)
        softmax_stage(s_cur, vts, False)

    def last_step(t, s_cur):
        softmax_stage(s_cur, load_values(t), True)

    score_stage(0, s0_ref)

    def pair(p, carry):
        pipelined_step(2 * p, s0_ref, s1_ref)
        pipelined_step(2 * p + 1, s1_ref, s0_ref)
        return carry

    lax.fori_loop(0, lax.shift_right_logical(qi, 1), pair, 0)
    odd = (qi & 1) == 1

    @pl.when(odd)
    def _():
        pipelined_step(qi - 1, s0_ref, s1_ref)
        last_step(qi, s1_ref)

    @pl.when(jnp.logical_not(odd))
    def _():
        last_step(qi, s0_ref)

    outs = []
    for hh in range(heads):
        acc = acc_ref[hh]
        outs.append(acc[0:V_HEAD, :] / acc[V_HEAD:V_HEAD + 1, :])
    o_ref[0] = jnp.concatenate(outs, axis=0).T.astype(o_ref.dtype)


def _attn_call(qt, k, vt):
    b, s, _ = k.shape
    heads = ATTN_HEADS_PER_STEP
    assert (heads * V_HEAD) % LANES == 0 and N_HEADS % heads == 0
    groups = N_HEADS // heads
    tq = Q_TILE
    assert Q_TILE == KV_TILE
    return pl.pallas_call(
        _attn_kernel,
        out_shape=jax.ShapeDtypeStruct((b, s, ATTN_WIDTH), _BF16),
        grid=(b, groups, s // tq),
        in_specs=[pl.BlockSpec((1, heads * HEAD_PAD, tq), lambda i, g, j: (i, g, j)),
                  pl.BlockSpec((1, s, heads * HEAD_PAD), lambda i, g, j: (i, 0, g)),
                  pl.BlockSpec((1, s // KV_TILE, heads * V_HEAD, KV_TILE),
                               lambda i, g, j: (i, 0, g, 0))],
        out_specs=pl.BlockSpec((1, tq, heads * V_HEAD), lambda i, g, j: (i, j, g)),
        scratch_shapes=[pltpu.VMEM((heads, 1, tq), _F32),
                        pltpu.VMEM((heads, 2 * V_HEAD, tq), _F32),
                        pltpu.VMEM((heads, KV_TILE, tq), _F32),
                        pltpu.VMEM((heads, KV_TILE, tq), _F32)],
        compiler_params=pltpu.CompilerParams(
            dimension_semantics=("arbitrary", "arbitrary", "arbitrary"),
            vmem_limit_bytes=VMEM_LIMIT),
        name="attn",
    )(qt, k, vt)


def _post_kernel(x_ref, attn_ref, conv_ref, p_ref, gattn_ref, wo_ref, gmlp_ref, wup_ref,
                 wdown_ref, gple_ref, wgate_ref, wple_ref, o_ref):
    an = _rms(attn_ref[...].astype(_F32), gattn_ref[...], ATTN_WIDTH).astype(_BF16)
    x = x_ref[...] + _dot(an, wo_ref[0:ATTN_WIDTH, :]) + _dot(conv_ref[...], wo_ref[ATTN_WIDTH:, :])

    h2 = _rms(x, gmlp_ref[...], D_MODEL).astype(_BF16)
    mlp = None
    for c in range(D_FF // FF_CHUNK):
        up = _dot(h2, wup_ref[:, c * FF_CHUNK:(c + 1) * FF_CHUNK])
        act = jnp.square(jnp.maximum(up, 0.0)).astype(_BF16)
        part = _dot(act, wdown_ref[c * FF_CHUNK:(c + 1) * FF_CHUNK, :])
        mlp = part if mlp is None else mlp + part
    x = x + mlp

    gate = jax.nn.sigmoid(_dot(_rms(x, gple_ref[...], D_MODEL).astype(_BF16), wgate_ref[...]))
    o_ref[...] = x + gate * _dot(p_ref[...].astype(_BF16), wple_ref[...])


def _post_call(layer, x, attn, conv, p, w):
    n = x.shape[0]
    t = ROW_TILE

    def row(width):
        return pl.BlockSpec((t, width), lambda i: (i, 0))

    def per_layer(arr):
        return pl.BlockSpec((None,) + arr.shape[1:], lambda i: (layer,) + (0,) * (arr.ndim - 1),
                            pipeline_mode=pl.Buffered(1))

    params = [w["g_out_attn"], w["w_o"], w["g_mlp"], w["w_up"], w["w_down"], w["g_ple"],
              w["w_ple_gate"], w["w_ple"]]
    return pl.pallas_call(
        _post_kernel,
        out_shape=jax.ShapeDtypeStruct((n, D_MODEL), _F32),
        grid=(n // t,),
        in_specs=[row(D_MODEL), row(ATTN_WIDTH), row(CONV_WIDTH),
                  pl.BlockSpec((None, t, PLE_DIM), lambda i: (layer, i, 0))]
                 + [per_layer(a) for a in params],
        out_specs=row(D_MODEL),
        compiler_params=pltpu.CompilerParams(
            dimension_semantics=("arbitrary",), vmem_limit_bytes=VMEM_LIMIT),
        name="post",
    )(x, attn, conv, p, *params)


def _prepare_weights(g_mix, w_in, g_q_lat, w_uq, g_kv_lat, w_ukv, g_qn_nope, g_qn_rope,
                     g_kn_nope, g_kn_rope, conv_w, g_out_attn, g_out_conv, w_o, g_mlp, w_up,
                     w_down, g_ple, w_ple_gate, w_ple):
    depth = w_in.shape[0]

    def gain(g):
        return g.reshape(depth, 1, -1).astype(_F32)

    def zeros(*shape):
        return jnp.zeros((depth,) + shape, _F32)

    o_kpe = Q_LORA + KV_LORA
    w_in = w_in.astype(_BF16)
    zcols = jnp.zeros((depth, D_MODEL, LANES), _BF16)
    w_in_p = jnp.concatenate(
        [w_in[..., :o_kpe], zcols[..., :ROPE_LO], w_in[..., o_kpe:o_kpe + QK_ROPE],
         zcols[..., :LANES - ROPE_HI], w_in[..., o_kpe + QK_ROPE:]], axis=-1)
    w_uq_p = jnp.pad(w_uq.astype(_BF16).reshape(depth, Q_LORA, N_HEADS, QK_HEAD),
                     ((0, 0), (0, 0), (0, 0), (0, HEAD_PAD - QK_HEAD))).reshape(depth, Q_LORA, QK_WIDTH)
    w_ukv_h = w_ukv.astype(_BF16).reshape(depth, KV_LORA, N_HEADS, QK_NOPE + V_HEAD)
    w_uk_p = jnp.pad(w_ukv_h[..., :QK_NOPE],
                     ((0, 0), (0, 0), (0, 0), (0, HEAD_PAD - QK_NOPE))).reshape(depth, KV_LORA, QK_WIDTH)
    w_uvt = jnp.swapaxes(w_ukv_h[..., QK_NOPE:].reshape(depth, KV_LORA, ATTN_WIDTH), 1, 2)
    q_scale = QK_HEAD ** -0.5 * LOG2_E
    g_qn = jnp.concatenate([g_qn_nope, g_qn_rope, zeros(LANES - ROPE_HI)], axis=-1) * q_scale
    g_qn_t = jnp.broadcast_to(g_qn[:, :, None], (depth, HEAD_PAD, ROW_TILE)).astype(_F32)
    return {
        "g_mix": gain(g_mix), "w_in": w_in_p,
        "g_q_lat": gain(g_q_lat), "w_uqt": jnp.swapaxes(w_uq_p, 1, 2),
        "g_kv_lat": gain(g_kv_lat), "w_uk": w_uk_p, "w_uvt": w_uvt,
        "g_qn_t": g_qn_t,
        "g_kn": gain(jnp.concatenate([g_kn_nope, zeros(LANES - QK_NOPE)], axis=-1)),
        "g_kpe": gain(jnp.concatenate([zeros(ROPE_LO), g_kn_rope, zeros(LANES - ROPE_HI)], axis=-1)),
        "conv_w": conv_w.astype(_F32), "g_out_conv": gain(g_out_conv),
        "g_out_attn": gain(g_out_attn), "w_o": w_o.astype(_BF16), "g_mlp": gain(g_mlp),
        "w_up": w_up.astype(_BF16), "w_down": w_down.astype(_BF16), "g_ple": gain(g_ple),
        "w_ple_gate": w_ple_gate.astype(_BF16), "w_ple": w_ple.astype(_BF16),
    }


def kernel(x, p, positions, g_mix, w_in, g_q_lat, w_uq, g_kv_lat, w_ukv, g_qn_nope, g_qn_rope,
           g_kn_nope, g_kn_rope, conv_w, g_out_attn, g_out_conv, w_o, g_mlp, w_up, w_down, g_ple,
           w_ple_gate, w_ple):
    b, s, d = x.shape
    depth = w_in.shape[0]
    assert d == D_MODEL and s % ROW_TILE == 0 and s % Q_TILE == 0
    w = _prepare_weights(g_mix, w_in, g_q_lat, w_uq, g_kv_lat, w_ukv, g_qn_nope, g_qn_rope,
                         g_kn_nope, g_kn_rope, conv_w, g_out_attn, g_out_conv, w_o, g_mlp, w_up,
                         w_down, g_ple, w_ple_gate, w_ple)
    tables = _rope_tables(positions)
    p_rows = p.reshape(depth, b * s, PLE_DIM)
    for layer in range(depth):
        qt, k, vt, conv = _proj_call(layer, x, tables, w)
        attn = _attn_call(qt, k, vt)
        x = _post_call(layer, x.reshape(b * s, d), attn.reshape(b * s, ATTN_WIDTH),
                       conv.reshape(b * s, CONV_WIDTH), p_rows, w).reshape(b, s, d)
    return x
```

```python
import functools

import jax
import jax.numpy as jnp
from jax import lax
from jax.experimental import pallas as pl
from jax.experimental.pallas import tpu as pltpu

D_MODEL = 1024
N_HEADS = 8
QK_NOPE = 64
QK_ROPE = 32
QK_HEAD = QK_NOPE + QK_ROPE
V_HEAD = 64
Q_LORA = 384
KV_LORA = 256
ATTN_WIDTH = N_HEADS * V_HEAD
CONV_WIDTH = D_MODEL - ATTN_WIDTH
CONV_TAPS = 3
D_FF = 4 * D_MODEL
PLE_DIM = 256
ROPE_THETA = 10000.0
EPS = 1e-6
LOG2_E = 1.4426950408889634

LANES = 128
SUBLANES = 8
HEAD_PAD = LANES
QK_WIDTH = N_HEADS * HEAD_PAD
ROPE_LO = QK_NOPE
ROPE_MID = QK_NOPE + QK_ROPE // 2
ROPE_HI = QK_NOPE + QK_ROPE
C_Q = 0
C_KV = C_Q + Q_LORA
C_KPE = C_KV + KV_LORA
C_B = C_KPE + LANES
C_C = C_B + CONV_WIDTH
C_X = C_C + CONV_WIDTH
IN_COLS_PAD = C_X + CONV_WIDTH

ROW_TILE = 512
Q_TILE = 512
KV_TILE = 512
ATTN_HEADS_PER_STEP = 4
FF_CHUNK = 1024
VMEM_LIMIT = 56 * 1024 * 1024

_BF16 = jnp.bfloat16
_F32 = jnp.float32


def _dot(a, b):
    return jnp.dot(a, b, preferred_element_type=_F32)


def _rms(x, g, n):
    ms = jnp.sum(x * x, axis=-1, keepdims=True) * (1.0 / n)
    return x * lax.rsqrt(ms + EPS) * g


def _rope_partner(t, lane):
    return jnp.where(lane < ROPE_MID, pltpu.roll(t, LANES - QK_ROPE // 2, axis=1),
                     pltpu.roll(t, QK_ROPE // 2, axis=1))


def _rope_table_kernel(pos_col_ref, pos_row_ref, invf_ref, sign_ref, invf_t_ref,
                       cos_ref, sin_ref, cos_t_ref, sin_t_ref):
    ang = pos_col_ref[0].astype(_F32) * invf_ref[...]
    cos_ref[0] = jnp.cos(ang)
    sin_ref[0] = jnp.sin(ang) * sign_ref[...]
    ang_t = pos_row_ref[0].astype(_F32) * invf_t_ref[...]
    cos_t_ref[0] = jnp.cos(ang_t)
    sin_t_ref[0] = jnp.sin(ang_t)


def _rope_tables(positions):
    b, s = positions.shape
    half = QK_ROPE // 2
    inv_freq = 1.0 / (ROPE_THETA ** (jnp.arange(0, QK_ROPE, 2, dtype=_F32) / QK_ROPE))
    zeros = jnp.zeros((half,), _F32)
    invf = jnp.concatenate([jnp.zeros((ROPE_LO,), _F32), inv_freq, inv_freq,
                            jnp.zeros((LANES - ROPE_HI,), _F32)]).reshape(1, LANES)
    sign = jnp.concatenate([jnp.zeros((ROPE_LO,), _F32), zeros - 1.0, zeros + 1.0,
                            jnp.zeros((LANES - ROPE_HI,), _F32)]).reshape(1, LANES)
    t = ROW_TILE
    invf_t = jnp.broadcast_to(inv_freq.reshape(half, 1), (half, t))
    const = pl.BlockSpec((1, LANES), lambda i, j: (0, 0))
    out = pl.BlockSpec((1, t, LANES), lambda i, j: (i, j, 0))
    out_t = pl.BlockSpec((1, half, t), lambda i, j: (i, 0, j))
    return pl.pallas_call(
        _rope_table_kernel,
        out_shape=(jax.ShapeDtypeStruct((b, s, LANES), _F32),) * 2
        + (jax.ShapeDtypeStruct((b, half, s), _F32),) * 2,
        grid=(b, s // t),
        in_specs=[pl.BlockSpec((1, t, 1), lambda i, j: (i, j, 0)),
                  pl.BlockSpec((1, 1, t), lambda i, j: (i, 0, j)), const, const,
                  pl.BlockSpec((half, t), lambda i, j: (0, 0))],
        out_specs=(out, out, out_t, out_t),
        name="rope_tables",
    )(positions.reshape(b, s, 1), positions.reshape(b, 1, s), invf, sign, invf_t)


def _proj_kernel(x_ref, cos_ref, sin_ref, cos_t_ref, sin_t_ref, gmix_ref, win_ref, gq_ref,
                 wuqt_ref, gkv_ref, wuk_ref, wuvt_ref, gqn_t_ref, gkn_ref, gkpe_ref, convw_ref,
                 gconv_ref, qt_out, k_out, vt_out, conv_out, u_ref):
    t = x_ref.shape[1]
    lane = lax.broadcasted_iota(jnp.int32, (1, LANES), 1)
    cos = cos_ref[0]
    sin = sin_ref[0]
    nt = (((1,), (1,)), ((), ()))

    @pl.when(pl.program_id(1) == 0)
    def _():
        u_ref[0:SUBLANES, :] = jnp.zeros((SUBLANES, CONV_WIDTH), _F32)

    h = _rms(x_ref[0], gmix_ref[...], D_MODEL).astype(_BF16)
    q_lat = _dot(h, win_ref[:, C_Q:C_KV])
    kv_lat = _dot(h, win_ref[:, C_KV:C_KPE])
    kpe = _dot(h, win_ref[:, C_KPE:C_B])
    u_c = _dot(h, win_ref[:, C_C:C_X])

    qn = _rms(q_lat, gq_ref[...], Q_LORA).astype(_BF16)
    kvn = _rms(kv_lat, gkv_ref[...], KV_LORA).astype(_BF16)
    u_x = _dot(h, win_ref[:, C_X:IN_COLS_PAD])
    ss_pe = jnp.sum(kpe * kpe, axis=-1, keepdims=True)
    kpe_n = kpe * lax.rsqrt(ss_pe * (1.0 / QK_ROPE) + EPS) * gkpe_ref[...]
    kpe_r = kpe_n * cos + _rope_partner(kpe_n, lane) * sin
    qt = lax.dot_general(wuqt_ref[...], qn, nt, preferred_element_type=_F32)
    u = u_c * u_x
    u_ref[SUBLANES:SUBLANES + t, :] = u
    kf = _dot(kvn, wuk_ref[...])

    cos_t = cos_t_ref[0]
    sin_t = sin_t_ref[0]
    half = QK_ROPE // 2
    for hd in range(N_HEADS):
        r0 = hd * HEAD_PAD
        nope = qt[r0:r0 + ROPE_LO, :]
        rope = qt[r0 + ROPE_LO:r0 + ROPE_HI, :]
        ss_n = jnp.sum(nope * nope, axis=0, keepdims=True)
        ss_r = jnp.sum(rope * rope, axis=0, keepdims=True)
        nope = nope * lax.rsqrt(ss_n * (1.0 / QK_NOPE) + EPS) * gqn_t_ref[0:ROPE_LO, :]
        rope = rope * lax.rsqrt(ss_r * (1.0 / QK_ROPE) + EPS) * gqn_t_ref[ROPE_LO:ROPE_HI, :]
        x1 = rope[0:half, :]
        x2 = rope[half:QK_ROPE, :]
        qt_out[0, r0:r0 + ROPE_LO, :] = nope.astype(_BF16)
        qt_out[0, r0 + ROPE_LO:r0 + ROPE_MID, :] = (x1 * cos_t - x2 * sin_t).astype(_BF16)
        qt_out[0, r0 + ROPE_MID:r0 + ROPE_HI, :] = (x2 * cos_t + x1 * sin_t).astype(_BF16)
        qt_out[0, r0 + ROPE_HI:r0 + HEAD_PAD, :] = jnp.zeros((HEAD_PAD - ROPE_HI, t), _BF16)

    gate_b = _dot(h, win_ref[:, C_B:C_C])
    vt_out[0, 0] = lax.dot_general(wuvt_ref[...], kvn, nt,
                                   preferred_element_type=_F32).astype(_BF16)

    rows = t // N_HEADS
    for hd in range(N_HEADS):
        th = kf[:, hd * HEAD_PAD:(hd + 1) * HEAD_PAD]
        ss = jnp.sum(th * th, axis=-1, keepdims=True)
        kh = th * lax.rsqrt(ss * (1.0 / QK_NOPE) + EPS) * gkn_ref[...] + kpe_r
        k_out[0, :, hd * HEAD_PAD:(hd + 1) * HEAD_PAD] = kh.astype(_BF16)

        a = hd * rows
        y = convw_ref[0:1, :] * u[a:a + rows, :]
        for j in range(1, CONV_TAPS):
            y = y + convw_ref[j:j + 1, :] * u_ref[SUBLANES - j + a:SUBLANES - j + a + rows, :]
        conv_out[0, a:a + rows, :] = _rms(gate_b[a:a + rows, :] * y, gconv_ref[...],
                                          CONV_WIDTH).astype(_BF16)
    u_ref[0:SUBLANES, :] = u[t - SUBLANES:t, :]


def _proj_call(layer, x, tables, w):
    b, s, _ = x.shape
    t = ROW_TILE
    cos, sin, cos_t, sin_t = tables

    def row(width):
        return pl.BlockSpec((1, t, width), lambda i, j: (i, j, 0))

    def col(height):
        return pl.BlockSpec((1, height, t), lambda i, j: (i, 0, j))

    def per_layer(arr):
        return pl.BlockSpec((None,) + arr.shape[1:], lambda i, j: (layer,) + (0,) * (arr.ndim - 1))

    params = [w["g_mix"], w["w_in"], w["g_q_lat"], w["w_uqt"], w["g_kv_lat"], w["w_uk"],
              w["w_uvt"], w["g_qn_t"], w["g_kn"], w["g_kpe"], w["conv_w"], w["g_out_conv"]]
    assert t == KV_TILE
    return pl.pallas_call(
        _proj_kernel,
        out_shape=(jax.ShapeDtypeStruct((b, QK_WIDTH, s), _BF16),
                   jax.ShapeDtypeStruct((b, s, QK_WIDTH), _BF16),
                   jax.ShapeDtypeStruct((b, s // t, ATTN_WIDTH, t), _BF16),
                   jax.ShapeDtypeStruct((b, s, CONV_WIDTH), _BF16)),
        grid=(b, s // t),
        in_specs=[row(D_MODEL), row(LANES), row(LANES), col(QK_ROPE // 2), col(QK_ROPE // 2)]
                 + [per_layer(a) for a in params],
        out_specs=(col(QK_WIDTH), row(QK_WIDTH),
                   pl.BlockSpec((1, 1, ATTN_WIDTH, t), lambda i, j: (i, j, 0, 0)),
                   row(CONV_WIDTH)),
        scratch_shapes=[pltpu.VMEM((SUBLANES + t, CONV_WIDTH), _F32)],
        compiler_params=pltpu.CompilerParams(
            dimension_semantics=("arbitrary", "arbitrary"), vmem_limit_bytes=VMEM_LIMIT),
        name="proj",
    )(x, cos, sin, cos_t, sin_t, *params)


def _attn_kernel(qt_ref, qt_next_ref, k_ref, vt_ref, o_ref, m_ref, acc_ref, s0_ref, s1_ref,
                 xa_ref, xb_ref):
    tq = qt_ref.shape[2]
    tk = KV_TILE
    qi = pl.program_id(2)
    heads = qt_ref.shape[1] // HEAD_PAD

    m_ref[...] = jnp.full(m_ref.shape, -jnp.inf, _F32)
    acc_ref[...] = jnp.zeros(acc_ref.shape, _F32)
    ones = jnp.ones((V_HEAD, tk), _BF16)

    def load_values(t):
        return [vt_ref[0, t, hh * V_HEAD:(hh + 1) * V_HEAD, :] for hh in range(heads)]

    def load_keys(t):
        start = t * tk if isinstance(t, int) else pl.multiple_of(t * tk, tk)
        return [k_ref[0, pl.ds(start, tk), hh * HEAD_PAD:(hh + 1) * HEAD_PAD]
                for hh in range(heads)]

    def scores(hh, kbs, s_ref, q_ref):
        qt = q_ref[0, hh * HEAD_PAD:(hh + 1) * HEAD_PAD, :]
        s_ref[hh] = _dot(kbs[hh], qt)

    def softmax_values(hh, s_ref, vts, masked):
        st = s_ref[hh]
        if masked:
            key = lax.broadcasted_iota(jnp.int32, (tk, tq), 0)
            qry = lax.broadcasted_iota(jnp.int32, (tk, tq), 1)
            st = jnp.where(key <= qry, st, -jnp.inf)
        m_old = m_ref[hh]
        m_new = jnp.maximum(m_old, jnp.max(st, axis=0, keepdims=True))
        alpha = jnp.exp2(m_old - m_new)
        pt = jnp.exp2(st - m_new).astype(_BF16)
        vt = jnp.concatenate([vts[hh], ones], axis=0)
        acc_ref[hh] = alpha * acc_ref[hh] + _dot(vt, pt)
        m_ref[hh] = m_new

    def score_stage(kbs, s_ref, q_ref=qt_ref):
        for hh in range(heads):
            scores(hh, kbs, s_ref, q_ref)

    def fused_step(t_cur, s_cur, masked, t_next, s_next, q_next):
        vts = load_values(t_cur)
        kbs = load_keys(t_next)
        for hh in range(heads):
            scores(hh, kbs, s_next, q_next)
            softmax_values(hh, s_cur, vts, masked)

    def pipelined_step(t, s_cur, s_next):
        fused_step(t, s_cur, False, t + 1, s_next, qt_ref)

    def last_step(t, s_cur, x_out):
        fused_step(t, s_cur, True, 0, x_out, qt_next_ref)
        outs = []
        for hh in range(heads):
            acc = acc_ref[hh]
            outs.append(acc[0:V_HEAD, :] / acc[V_HEAD:V_HEAD + 1, :])
        o_ref[0] = jnp.concatenate(outs, axis=0).T.astype(o_ref.dtype)

    def middle_steps(x_in):
        pipelined_step(0, x_in, s0_ref)

        def pair(p, carry):
            pipelined_step(2 * p + 1, s0_ref, s1_ref)
            pipelined_step(2 * p + 2, s1_ref, s0_ref)
            return carry

        lax.fori_loop(0, lax.shift_right_logical(qi - 1, 1), pair, 0)

    odd = (qi & 1) == 1

    @pl.when(qi == 0)
    def _():
        score_stage(load_keys(0), xa_ref)
        last_step(0, xa_ref, xb_ref)

    @pl.when(jnp.logical_and(qi > 0, jnp.logical_not(odd)))
    def _():
        middle_steps(xa_ref)
        pipelined_step(qi - 1, s0_ref, s1_ref)
        last_step(qi, s1_ref, xb_ref)

    @pl.when(odd)
    def _():
        middle_steps(xb_ref)
        last_step(qi, s0_ref, xa_ref)


def _attn_call(qt, k, vt):
    b, s, _ = k.shape
    heads = ATTN_HEADS_PER_STEP
    assert (heads * V_HEAD) % LANES == 0 and N_HEADS % heads == 0
    groups = N_HEADS // heads
    tq = Q_TILE
    assert Q_TILE == KV_TILE
    return pl.pallas_call(
        _attn_kernel,
        out_shape=jax.ShapeDtypeStruct((b, s, ATTN_WIDTH), _BF16),
        grid=(b, groups, s // tq),
        in_specs=[pl.BlockSpec((1, heads * HEAD_PAD, tq), lambda i, g, j: (i, g, j)),
                  pl.BlockSpec((1, heads * HEAD_PAD, tq),
                               lambda i, g, j: (i, g, jnp.minimum(j + 1, s // tq - 1))),
                  pl.BlockSpec((1, s, heads * HEAD_PAD), lambda i, g, j: (i, 0, g)),
                  pl.BlockSpec((1, s // KV_TILE, heads * V_HEAD, KV_TILE),
                               lambda i, g, j: (i, 0, g, 0))],
        out_specs=pl.BlockSpec((1, tq, heads * V_HEAD), lambda i, g, j: (i, j, g)),
        scratch_shapes=[pltpu.VMEM((heads, 1, tq), _F32),
                        pltpu.VMEM((heads, 2 * V_HEAD, tq), _F32),
                        ] + [pltpu.VMEM((heads, KV_TILE, tq), _F32)] * 4,
        compiler_params=pltpu.CompilerParams(
            dimension_semantics=("arbitrary", "arbitrary", "arbitrary"),
            vmem_limit_bytes=VMEM_LIMIT),
        name="attn",
    )(qt, qt, k, vt)


def _post_kernel(x_ref, attn_ref, conv_ref, p_ref, gattn_ref, wo_ref, gmlp_ref, wup_ref,
                 wdown_ref, gple_ref, wgate_ref, wple_ref, o_ref):
    an = _rms(attn_ref[...].astype(_F32), gattn_ref[...], ATTN_WIDTH).astype(_BF16)
    x = x_ref[...] + _dot(an, wo_ref[0:ATTN_WIDTH, :]) + _dot(conv_ref[...], wo_ref[ATTN_WIDTH:, :])

    h2 = _rms(x, gmlp_ref[...], D_MODEL).astype(_BF16)
    mlp = None
    for c in range(D_FF // FF_CHUNK):
        up = _dot(h2, wup_ref[:, c * FF_CHUNK:(c + 1) * FF_CHUNK])
        act = jnp.square(jnp.maximum(up, 0.0)).astype(_BF16)
        part = _dot(act, wdown_ref[c * FF_CHUNK:(c + 1) * FF_CHUNK, :])
        mlp = part if mlp is None else mlp + part
    x = x + mlp

    gate = jax.nn.sigmoid(_dot(_rms(x, gple_ref[...], D_MODEL).astype(_BF16), wgate_ref[...]))
    o_ref[...] = x + gate * _dot(p_ref[...].astype(_BF16), wple_ref[...])


def _post_call(layer, x, attn, conv, p, w):
    n = x.shape[0]
    t = ROW_TILE

    def row(width):
        return pl.BlockSpec((t, width), lambda i: (i, 0))

    def per_layer(arr):
        return pl.BlockSpec((None,) + arr.shape[1:], lambda i: (layer,) + (0,) * (arr.ndim - 1),
                            pipeline_mode=pl.Buffered(1))

    params = [w["g_out_attn"], w["w_o"], w["g_mlp"], w["w_up"], w["w_down"], w["g_ple"],
              w["w_ple_gate"], w["w_ple"]]
    return pl.pallas_call(
        _post_kernel,
        out_shape=jax.ShapeDtypeStruct((n, D_MODEL), _F32),
        grid=(n // t,),
        in_specs=[row(D_MODEL), row(ATTN_WIDTH), row(CONV_WIDTH),
                  pl.BlockSpec((None, t, PLE_DIM), lambda i: (layer, i, 0))]
                 + [per_layer(a) for a in params],
        out_specs=row(D_MODEL),
        compiler_params=pltpu.CompilerParams(
            dimension_semantics=("arbitrary",), vmem_limit_bytes=VMEM_LIMIT),
        name="post",
    )(x, attn, conv, p, *params)


def _prepare_weights(g_mix, w_in, g_q_lat, w_uq, g_kv_lat, w_ukv, g_qn_nope, g_qn_rope,
                     g_kn_nope, g_kn_rope, conv_w, g_out_attn, g_out_conv, w_o, g_mlp, w_up,
                     w_down, g_ple, w_ple_gate, w_ple):
    depth = w_in.shape[0]

    def gain(g):
        return g.reshape(depth, 1, -1).astype(_F32)

    def zeros(*shape):
        return jnp.zeros((depth,) + shape, _F32)

    o_kpe = Q_LORA + KV_LORA
    w_in = w_in.astype(_BF16)
    zcols = jnp.zeros((depth, D_MODEL, LANES), _BF16)
    w_in_p = jnp.concatenate(
        [w_in[..., :o_kpe], zcols[..., :ROPE_LO], w_in[..., o_kpe:o_kpe + QK_ROPE],
         zcols[..., :LANES - ROPE_HI], w_in[..., o_kpe + QK_ROPE:]], axis=-1)
    w_uq_p = jnp.pad(w_uq.astype(_BF16).reshape(depth, Q_LORA, N_HEADS, QK_HEAD),
                     ((0, 0), (0, 0), (0, 0), (0, HEAD_PAD - QK_HEAD))).reshape(depth, Q_LORA, QK_WIDTH)
    w_ukv_h = w_ukv.astype(_BF16).reshape(depth, KV_LORA, N_HEADS, QK_NOPE + V_HEAD)
    w_uk_p = jnp.pad(w_ukv_h[..., :QK_NOPE],
                     ((0, 0), (0, 0), (0, 0), (0, HEAD_PAD - QK_NOPE))).reshape(depth, KV_LORA, QK_WIDTH)
    w_uvt = jnp.swapaxes(w_ukv_h[..., QK_NOPE:].reshape(depth, KV_LORA, ATTN_WIDTH), 1, 2)
    q_scale = QK_HEAD ** -0.5 * LOG2_E
    g_qn = jnp.concatenate([g_qn_nope, g_qn_rope, zeros(LANES - ROPE_HI)], axis=-1) * q_scale
    g_qn_t = jnp.broadcast_to(g_qn[:, :, None], (depth, HEAD_PAD, ROW_TILE)).astype(_F32)
    return {
        "g_mix": gain(g_mix), "w_in": w_in_p,
        "g_q_lat": gain(g_q_lat), "w_uqt": jnp.swapaxes(w_uq_p, 1, 2),
        "g_kv_lat": gain(g_kv_lat), "w_uk": w_uk_p, "w_uvt": w_uvt,
        "g_qn_t": g_qn_t,
        "g_kn": gain(jnp.concatenate([g_kn_nope, zeros(LANES - QK_NOPE)], axis=-1)),
        "g_kpe": gain(jnp.concatenate([zeros(ROPE_LO), g_kn_rope, zeros(LANES - ROPE_HI)], axis=-1)),
        "conv_w": conv_w.astype(_F32), "g_out_conv": gain(g_out_conv),
        "g_out_attn": gain(g_out_attn), "w_o": w_o.astype(_BF16), "g_mlp": gain(g_mlp),
        "w_up": w_up.astype(_BF16), "w_down": w_down.astype(_BF16), "g_ple": gain(g_ple),
        "w_ple_gate": w_ple_gate.astype(_BF16), "w_ple": w_ple.astype(_BF16),
    }


def kernel(x, p, positions, g_mix, w_in, g_q_lat, w_uq, g_kv_lat, w_ukv, g_qn_nope, g_qn_rope,
           g_kn_nope, g_kn_rope, conv_w, g_out_attn, g_out_conv, w_o, g_mlp, w_up, w_down, g_ple,
           w_ple_gate, w_ple):
    b, s, d = x.shape
    depth = w_in.shape[0]
    assert d == D_MODEL and s % ROW_TILE == 0 and s % Q_TILE == 0
    w = _prepare_weights(g_mix, w_in, g_q_lat, w_uq, g_kv_lat, w_ukv, g_qn_nope, g_qn_rope,
                         g_kn_nope, g_kn_rope, conv_w, g_out_attn, g_out_conv, w_o, g_mlp, w_up,
                         w_down, g_ple, w_ple_gate, w_ple)
    tables = _rope_tables(positions)
    p_rows = p.reshape(depth, b * s, PLE_DIM)
    for layer in range(depth):
        qt, k, vt, conv = _proj_call(layer, x, tables, w)
        attn = _attn_call(qt, k, vt)
        x = _post_call(layer, x.reshape(b * s, d), attn.reshape(b * s, ATTN_WIDTH),
                       conv.reshape(b * s, CONV_WIDTH), p_rows, w).reshape(b, s, d)
    return x
```

```python
import functools

import jax
import jax.numpy as jnp
from jax import lax
from jax.experimental import pallas as pl
from jax.experimental.pallas import tpu as pltpu

D_MODEL = 1024
N_HEADS = 8
QK_NOPE = 64
QK_ROPE = 32
QK_HEAD = QK_NOPE + QK_ROPE
V_HEAD = 64
Q_LORA = 384
KV_LORA = 256
ATTN_WIDTH = N_HEADS * V_HEAD
CONV_WIDTH = D_MODEL - ATTN_WIDTH
CONV_TAPS = 3
D_FF = 4 * D_MODEL
PLE_DIM = 256
ROPE_THETA = 10000.0
EPS = 1e-6
LOG2_E = 1.4426950408889634

LANES = 128
SUBLANES = 8
HEAD_PAD = LANES
QK_WIDTH = N_HEADS * HEAD_PAD
ROPE_LO = QK_NOPE
ROPE_MID = QK_NOPE + QK_ROPE // 2
ROPE_HI = QK_NOPE + QK_ROPE
C_Q = 0
C_KV = C_Q + Q_LORA
C_KPE = C_KV + KV_LORA
C_B = C_KPE + LANES
C_C = C_B + CONV_WIDTH
C_X = C_C + CONV_WIDTH
IN_COLS_PAD = C_X + CONV_WIDTH

ROW_TILE = 512
Q_TILE = 512
KV_TILE = 512
ATTN_HEADS_PER_STEP = 4
FF_CHUNK = 1024
POST_TILE = 512
POST_PARTS = 2
VMEM_LIMIT = 56 * 1024 * 1024

_BF16 = jnp.bfloat16
_F32 = jnp.float32


def _dot(a, b):
    return jnp.dot(a, b, preferred_element_type=_F32)


def _rms(x, g, n):
    ms = jnp.sum(x * x, axis=-1, keepdims=True) * (1.0 / n)
    return x * lax.rsqrt(ms + EPS) * g


def _rope_partner(t, lane):
    return jnp.where(lane < ROPE_MID, pltpu.roll(t, LANES - QK_ROPE // 2, axis=1),
                     pltpu.roll(t, QK_ROPE // 2, axis=1))


def _rope_table_kernel(pos_col_ref, pos_row_ref, invf_ref, sign_ref, invf_t_ref,
                       cos_ref, sin_ref, cos_t_ref, sin_t_ref):
    ang = pos_col_ref[0].astype(_F32) * invf_ref[...]
    cos_ref[0] = jnp.cos(ang)
    sin_ref[0] = jnp.sin(ang) * sign_ref[...]
    ang_t = pos_row_ref[0].astype(_F32) * invf_t_ref[...]
    cos_t_ref[0] = jnp.cos(ang_t)
    sin_t_ref[0] = jnp.sin(ang_t)


def _rope_tables(positions):
    b, s = positions.shape
    half = QK_ROPE // 2
    inv_freq = 1.0 / (ROPE_THETA ** (jnp.arange(0, QK_ROPE, 2, dtype=_F32) / QK_ROPE))
    zeros = jnp.zeros((half,), _F32)
    invf = jnp.concatenate([jnp.zeros((ROPE_LO,), _F32), inv_freq, inv_freq,
                            jnp.zeros((LANES - ROPE_HI,), _F32)]).reshape(1, LANES)
    sign = jnp.concatenate([jnp.zeros((ROPE_LO,), _F32), zeros - 1.0, zeros + 1.0,
                            jnp.zeros((LANES - ROPE_HI,), _F32)]).reshape(1, LANES)
    t = ROW_TILE
    invf_t = jnp.broadcast_to(inv_freq.reshape(half, 1), (half, t))
    const = pl.BlockSpec((1, LANES), lambda i, j: (0, 0))
    out = pl.BlockSpec((1, t, LANES), lambda i, j: (i, j, 0))
    out_t = pl.BlockSpec((1, half, t), lambda i, j: (i, 0, j))
    return pl.pallas_call(
        _rope_table_kernel,
        out_shape=(jax.ShapeDtypeStruct((b, s, LANES), _F32),) * 2
        + (jax.ShapeDtypeStruct((b, half, s), _F32),) * 2,
        grid=(b, s // t),
        in_specs=[pl.BlockSpec((1, t, 1), lambda i, j: (i, j, 0)),
                  pl.BlockSpec((1, 1, t), lambda i, j: (i, 0, j)), const, const,
                  pl.BlockSpec((half, t), lambda i, j: (0, 0))],
        out_specs=(out, out, out_t, out_t),
        name="rope_tables",
    )(positions.reshape(b, s, 1), positions.reshape(b, 1, s), invf, sign, invf_t)


def _proj_kernel(x_ref, cos_ref, sin_ref, cos_t_ref, sin_t_ref, gmix_ref, win_ref, gq_ref,
                 wuqt_ref, gkv_ref, wuk_ref, wuvt_ref, gqn_t_ref, gkn_ref, gkpe_ref, convw_ref,
                 gconv_ref, qt_out, k_out, vt_out, conv_out, u_ref):
    t = x_ref.shape[1]
    lane = lax.broadcasted_iota(jnp.int32, (1, LANES), 1)
    cos = cos_ref[0]
    sin = sin_ref[0]
    nt = (((1,), (1,)), ((), ()))

    @pl.when(pl.program_id(1) == 0)
    def _():
        u_ref[0:SUBLANES, :] = jnp.zeros((SUBLANES, CONV_WIDTH), _F32)

    h = _rms(x_ref[0], gmix_ref[...], D_MODEL).astype(_BF16)
    q_lat = _dot(h, win_ref[:, C_Q:C_KV])
    kv_lat = _dot(h, win_ref[:, C_KV:C_KPE])
    kpe = _dot(h, win_ref[:, C_KPE:C_B])
    u_c = _dot(h, win_ref[:, C_C:C_X])

    qn = _rms(q_lat, gq_ref[...], Q_LORA).astype(_BF16)
    kvn = _rms(kv_lat, gkv_ref[...], KV_LORA).astype(_BF16)
    u_x = _dot(h, win_ref[:, C_X:IN_COLS_PAD])
    ss_pe = jnp.sum(kpe * kpe, axis=-1, keepdims=True)
    kpe_n = kpe * lax.rsqrt(ss_pe * (1.0 / QK_ROPE) + EPS) * gkpe_ref[...]
    kpe_r = kpe_n * cos + _rope_partner(kpe_n, lane) * sin
    qt = lax.dot_general(wuqt_ref[...], qn, nt, preferred_element_type=_F32)
    u = u_c * u_x
    u_ref[SUBLANES:SUBLANES + t, :] = u
    kf = _dot(kvn, wuk_ref[...])

    cos_t = cos_t_ref[0]
    sin_t = sin_t_ref[0]
    half = QK_ROPE // 2
    for hd in range(N_HEADS):
        r0 = hd * HEAD_PAD
        nope = qt[r0:r0 + ROPE_LO, :]
        rope = qt[r0 + ROPE_LO:r0 + ROPE_HI, :]
        ss_n = jnp.sum(nope * nope, axis=0, keepdims=True)
        ss_r = jnp.sum(rope * rope, axis=0, keepdims=True)
        nope = nope * lax.rsqrt(ss_n * (1.0 / QK_NOPE) + EPS) * gqn_t_ref[0:ROPE_LO, :]
        rope = rope * lax.rsqrt(ss_r * (1.0 / QK_ROPE) + EPS) * gqn_t_ref[ROPE_LO:ROPE_HI, :]
        x1 = rope[0:half, :]
        x2 = rope[half:QK_ROPE, :]
        qt_out[0, r0:r0 + ROPE_LO, :] = nope.astype(_BF16)
        qt_out[0, r0 + ROPE_LO:r0 + ROPE_MID, :] = (x1 * cos_t - x2 * sin_t).astype(_BF16)
        qt_out[0, r0 + ROPE_MID:r0 + ROPE_HI, :] = (x2 * cos_t + x1 * sin_t).astype(_BF16)
        qt_out[0, r0 + ROPE_HI:r0 + HEAD_PAD, :] = jnp.zeros((HEAD_PAD - ROPE_HI, t), _BF16)

    gate_b = _dot(h, win_ref[:, C_B:C_C])
    vt_out[0, 0] = lax.dot_general(wuvt_ref[...], kvn, nt,
                                   preferred_element_type=_F32).astype(_BF16)

    rows = t // N_HEADS
    for hd in range(N_HEADS):
        th = kf[:, hd * HEAD_PAD:(hd + 1) * HEAD_PAD]
        ss = jnp.sum(th * th, axis=-1, keepdims=True)
        kh = th * lax.rsqrt(ss * (1.0 / QK_NOPE) + EPS) * gkn_ref[...] + kpe_r
        k_out[0, :, hd * HEAD_PAD:(hd + 1) * HEAD_PAD] = kh.astype(_BF16)

        a = hd * rows
        y = convw_ref[0:1, :] * u[a:a + rows, :]
        for j in range(1, CONV_TAPS):
            y = y + convw_ref[j:j + 1, :] * u_ref[SUBLANES - j + a:SUBLANES - j + a + rows, :]
        conv_out[0, a:a + rows, :] = _rms(gate_b[a:a + rows, :] * y, gconv_ref[...],
                                          CONV_WIDTH).astype(_BF16)
    u_ref[0:SUBLANES, :] = u[t - SUBLANES:t, :]


def _proj_call(layer, x, tables, w):
    b, s, _ = x.shape
    t = ROW_TILE
    cos, sin, cos_t, sin_t = tables

    def row(width):
        return pl.BlockSpec((1, t, width), lambda i, j: (i, j, 0))

    def col(height):
        return pl.BlockSpec((1, height, t), lambda i, j: (i, 0, j))

    def per_layer(arr):
        return pl.BlockSpec((None,) + arr.shape[1:], lambda i, j: (layer,) + (0,) * (arr.ndim - 1))

    params = [w["g_mix"], w["w_in"], w["g_q_lat"], w["w_uqt"], w["g_kv_lat"], w["w_uk"],
              w["w_uvt"], w["g_qn_t"], w["g_kn"], w["g_kpe"], w["conv_w"], w["g_out_conv"]]
    assert t == KV_TILE
    return pl.pallas_call(
        _proj_kernel,
        out_shape=(jax.ShapeDtypeStruct((b, QK_WIDTH, s), _BF16),
                   jax.ShapeDtypeStruct((b, s, QK_WIDTH), _BF16),
                   jax.ShapeDtypeStruct((b, s // t, ATTN_WIDTH, t), _BF16),
                   jax.ShapeDtypeStruct((b, s, CONV_WIDTH), _BF16)),
        grid=(b, s // t),
        in_specs=[row(D_MODEL), row(LANES), row(LANES), col(QK_ROPE // 2), col(QK_ROPE // 2)]
                 + [per_layer(a) for a in params],
        out_specs=(col(QK_WIDTH), row(QK_WIDTH),
                   pl.BlockSpec((1, 1, ATTN_WIDTH, t), lambda i, j: (i, j, 0, 0)),
                   row(CONV_WIDTH)),
        scratch_shapes=[pltpu.VMEM((SUBLANES + t, CONV_WIDTH), _F32)],
        compiler_params=pltpu.CompilerParams(
            dimension_semantics=("arbitrary", "arbitrary"), vmem_limit_bytes=VMEM_LIMIT),
        name="proj",
    )(x, cos, sin, cos_t, sin_t, *params)


def _attn_kernel(qt_ref, qt_next_ref, k_ref, vt_ref, o_ref, m_ref, acc_ref, s0_ref, s1_ref,
                 xa_ref, xb_ref):
    tq = qt_ref.shape[2]
    tk = KV_TILE
    qi = pl.program_id(2)
    heads = qt_ref.shape[1] // HEAD_PAD

    m_ref[...] = jnp.full(m_ref.shape, -jnp.inf, _F32)
    acc_ref[...] = jnp.zeros(acc_ref.shape, _F32)
    ones = jnp.ones((V_HEAD, tk), _BF16)

    def load_values(t):
        return [vt_ref[0, t, hh * V_HEAD:(hh + 1) * V_HEAD, :] for hh in range(heads)]

    def load_keys(t):
        start = t * tk if isinstance(t, int) else pl.multiple_of(t * tk, tk)
        return [k_ref[0, pl.ds(start, tk), hh * HEAD_PAD:(hh + 1) * HEAD_PAD]
                for hh in range(heads)]

    def scores(hh, kbs, s_ref, q_ref):
        qt = q_ref[0, hh * HEAD_PAD:(hh + 1) * HEAD_PAD, :]
        st = _dot(kbs[hh], qt)
        s_ref[hh, 0:tk, :] = st
        s_ref[hh, tk:tk + 1, :] = jnp.max(st, axis=0, keepdims=True)

    def softmax_values(hh, s_ref, vts, masked):
        st = s_ref[hh, 0:tk, :]
        if masked:
            key = lax.broadcasted_iota(jnp.int32, (tk, tq), 0)
            qry = lax.broadcasted_iota(jnp.int32, (tk, tq), 1)
            st = jnp.where(key <= qry, st, -jnp.inf)
            block_max = jnp.max(st, axis=0, keepdims=True)
        else:
            block_max = s_ref[hh, tk:tk + 1, :]
        m_old = m_ref[hh]
        m_new = jnp.maximum(m_old, block_max)
        alpha = jnp.exp2(m_old - m_new)
        pt = jnp.exp2(st - m_new).astype(_BF16)
        vt = jnp.concatenate([vts[hh], ones], axis=0)
        acc_ref[hh] = alpha * acc_ref[hh] + _dot(vt, pt)
        m_ref[hh] = m_new

    def score_stage(kbs, s_ref, q_ref=qt_ref):
        for hh in range(heads):
            scores(hh, kbs, s_ref, q_ref)

    def fused_step(t_cur, s_cur, masked, t_next, s_next, q_next):
        vts = load_values(t_cur)
        kbs = load_keys(t_next)
        for hh in range(heads):
            scores(hh, kbs, s_next, q_next)
            softmax_values(hh, s_cur, vts, masked)

    def pipelined_step(t, s_cur, s_next):
        fused_step(t, s_cur, False, t + 1, s_next, qt_ref)

    def last_step(t, s_cur, x_out):
        fused_step(t, s_cur, True, 0, x_out, qt_next_ref)
        outs = []
        for hh in range(heads):
            acc = acc_ref[hh]
            outs.append(acc[0:V_HEAD, :] / acc[V_HEAD:V_HEAD + 1, :])
        o_ref[0] = jnp.concatenate(outs, axis=0).T.astype(o_ref.dtype)

    def middle_steps(x_in):
        pipelined_step(0, x_in, s0_ref)

        def pair(p, carry):
            pipelined_step(2 * p + 1, s0_ref, s1_ref)
            pipelined_step(2 * p + 2, s1_ref, s0_ref)
            return carry

        lax.fori_loop(0, lax.shift_right_logical(qi - 1, 1), pair, 0)

    odd = (qi & 1) == 1

    @pl.when(qi == 0)
    def _():
        score_stage(load_keys(0), xa_ref)
        last_step(0, xa_ref, xb_ref)

    @pl.when(jnp.logical_and(qi > 0, jnp.logical_not(odd)))
    def _():
        middle_steps(xa_ref)
        pipelined_step(qi - 1, s0_ref, s1_ref)
        last_step(qi, s1_ref, xb_ref)

    @pl.when(odd)
    def _():
        middle_steps(xb_ref)
        last_step(qi, s0_ref, xa_ref)


def _attn_call(qt, k, vt):
    b, s, _ = k.shape
    heads = ATTN_HEADS_PER_STEP
    assert (heads * V_HEAD) % LANES == 0 and N_HEADS % heads == 0
    groups = N_HEADS // heads
    tq = Q_TILE
    assert Q_TILE == KV_TILE
    return pl.pallas_call(
        _attn_kernel,
        out_shape=jax.ShapeDtypeStruct((b, s, ATTN_WIDTH), _BF16),
        grid=(b, groups, s // tq),
        in_specs=[pl.BlockSpec((1, heads * HEAD_PAD, tq), lambda i, g, j: (i, g, j)),
                  pl.BlockSpec((1, heads * HEAD_PAD, tq),
                               lambda i, g, j: (i, g, jnp.minimum(j + 1, s // tq - 1))),
                  pl.BlockSpec((1, s, heads * HEAD_PAD), lambda i, g, j: (i, 0, g)),
                  pl.BlockSpec((1, s // KV_TILE, heads * V_HEAD, KV_TILE),
                               lambda i, g, j: (i, 0, g, 0))],
        out_specs=pl.BlockSpec((1, tq, heads * V_HEAD), lambda i, g, j: (i, j, g)),
        scratch_shapes=[pltpu.VMEM((heads, 1, tq), _F32),
                        pltpu.VMEM((heads, 2 * V_HEAD, tq), _F32),
                        ] + [pltpu.VMEM((heads, KV_TILE + SUBLANES, tq), _F32)] * 4,
        compiler_params=pltpu.CompilerParams(
            dimension_semantics=("arbitrary", "arbitrary", "arbitrary"),
            vmem_limit_bytes=VMEM_LIMIT),
        name="attn",
    )(qt, qt, k, vt)


def _post_kernel(x_ref, attn_ref, conv_ref, p_ref, gattn_ref, wo_ref, gmlp_ref, wup_ref,
                 wdown_ref, gple_ref, wgate_ref, wple_ref, o_ref):
    t = x_ref.shape[0]
    rows = t // POST_PARTS
    halves = [slice(i * rows, (i + 1) * rows) for i in range(POST_PARTS)]

    def out_proj(r):
        an = _rms(attn_ref[r, :].astype(_F32), gattn_ref[...], ATTN_WIDTH).astype(_BF16)
        return (x_ref[r, :] + _dot(an, wo_ref[0:ATTN_WIDTH, :])
                + _dot(conv_ref[r, :], wo_ref[ATTN_WIDTH:, :]))

    def mlp(x):
        h2 = _rms(x, gmlp_ref[...], D_MODEL).astype(_BF16)
        acc = None
        for c in range(D_FF // FF_CHUNK):
            up = _dot(h2, wup_ref[:, c * FF_CHUNK:(c + 1) * FF_CHUNK])
            act = jnp.square(jnp.maximum(up, 0.0)).astype(_BF16)
            part = _dot(act, wdown_ref[c * FF_CHUNK:(c + 1) * FF_CHUNK, :])
            acc = part if acc is None else acc + part
        return x + acc

    def embed(r):
        return _dot(p_ref[r, :].astype(_BF16), wple_ref[...])

    def gated_add(r, x, emb):
        gate = jax.nn.sigmoid(_dot(_rms(x, gple_ref[...], D_MODEL).astype(_BF16), wgate_ref[...]))
        o_ref[r, :] = x + gate * emb

    embs, xs = [], []
    for r in halves:
        embs.append(embed(r))
        xs.append(out_proj(r))
    xs = [mlp(x) for x in xs]
    for r, x, emb in zip(halves, xs, embs):
        gated_add(r, x, emb)


def _post_call(layer, x, attn, conv, p, w):
    n = x.shape[0]
    t = POST_TILE

    def row(width):
        return pl.BlockSpec((t, width), lambda i: (i, 0))

    def per_layer(arr):
        return pl.BlockSpec((None,) + arr.shape[1:], lambda i: (layer,) + (0,) * (arr.ndim - 1),
                            pipeline_mode=pl.Buffered(1))

    params = [w["g_out_attn"], w["w_o"], w["g_mlp"], w["w_up"], w["w_down"], w["g_ple"],
              w["w_ple_gate"], w["w_ple"]]
    return pl.pallas_call(
        _post_kernel,
        out_shape=jax.ShapeDtypeStruct((n, D_MODEL), _F32),
        grid=(n // t,),
        in_specs=[row(D_MODEL), row(ATTN_WIDTH), row(CONV_WIDTH),
                  pl.BlockSpec((None, t, PLE_DIM), lambda i: (layer, i, 0))]
                 + [per_layer(a) for a in params],
        out_specs=row(D_MODEL),
        compiler_params=pltpu.CompilerParams(
            dimension_semantics=("arbitrary",), vmem_limit_bytes=VMEM_LIMIT),
        name="post",
    )(x, attn, conv, p, *params)


def _prepare_weights(g_mix, w_in, g_q_lat, w_uq, g_kv_lat, w_ukv, g_qn_nope, g_qn_rope,
                     g_kn_nope, g_kn_rope, conv_w, g_out_attn, g_out_conv, w_o, g_mlp, w_up,
                     w_down, g_ple, w_ple_gate, w_ple):
    depth = w_in.shape[0]

    def gain(g):
        return g.reshape(depth, 1, -1).astype(_F32)

    def zeros(*shape):
        return jnp.zeros((depth,) + shape, _F32)

    o_kpe = Q_LORA + KV_LORA
    w_in = w_in.astype(_BF16)
    zcols = jnp.zeros((depth, D_MODEL, LANES), _BF16)
    w_in_p = jnp.concatenate(
        [w_in[..., :o_kpe], zcols[..., :ROPE_LO], w_in[..., o_kpe:o_kpe + QK_ROPE],
         zcols[..., :LANES - ROPE_HI], w_in[..., o_kpe + QK_ROPE:]], axis=-1)
    w_uq_p = jnp.pad(w_uq.astype(_BF16).reshape(depth, Q_LORA, N_HEADS, QK_HEAD),
                     ((0, 0), (0, 0), (0, 0), (0, HEAD_PAD - QK_HEAD))).reshape(depth, Q_LORA, QK_WIDTH)
    w_ukv_h = w_ukv.astype(_BF16).reshape(depth, KV_LORA, N_HEADS, QK_NOPE + V_HEAD)
    w_uk_p = jnp.pad(w_ukv_h[..., :QK_NOPE],
                     ((0, 0), (0, 0), (0, 0), (0, HEAD_PAD - QK_NOPE))).reshape(depth, KV_LORA, QK_WIDTH)
    w_uvt = jnp.swapaxes(w_ukv_h[..., QK_NOPE:].reshape(depth, KV_LORA, ATTN_WIDTH), 1, 2)
    q_scale = QK_HEAD ** -0.5 * LOG2_E
    g_qn = jnp.concatenate([g_qn_nope, g_qn_rope, zeros(LANES - ROPE_HI)], axis=-1) * q_scale
    g_qn_t = jnp.broadcast_to(g_qn[:, :, None], (depth, HEAD_PAD, ROW_TILE)).astype(_F32)
    return {
        "g_mix": gain(g_mix), "w_in": w_in_p,
        "g_q_lat": gain(g_q_lat), "w_uqt": jnp.swapaxes(w_uq_p, 1, 2),
        "g_kv_lat": gain(g_kv_lat), "w_uk": w_uk_p, "w_uvt": w_uvt,
        "g_qn_t": g_qn_t,
        "g_kn": gain(jnp.concatenate([g_kn_nope, zeros(LANES - QK_NOPE)], axis=-1)),
        "g_kpe": gain(jnp.concatenate([zeros(ROPE_LO), g_kn_rope, zeros(LANES - ROPE_HI)], axis=-1)),
        "conv_w": conv_w.astype(_F32), "g_out_conv": gain(g_out_conv),
        "g_out_attn": gain(g_out_attn), "w_o": w_o.astype(_BF16), "g_mlp": gain(g_mlp),
        "w_up": w_up.astype(_BF16), "w_down": w_down.astype(_BF16), "g_ple": gain(g_ple),
        "w_ple_gate": w_ple_gate.astype(_BF16), "w_ple": w_ple.astype(_BF16),
    }


def kernel(x, p, positions, g_mix, w_in, g_q_lat, w_uq, g_kv_lat, w_ukv, g_qn_nope, g_qn_rope,
           g_kn_nope, g_kn_rope, conv_w, g_out_attn, g_out_conv, w_o, g_mlp, w_up, w_down, g_ple,
           w_ple_gate, w_ple):
    b, s, d = x.shape
    depth = w_in.shape[0]
    assert d == D_MODEL and s % ROW_TILE == 0 and s % Q_TILE == 0
    w = _prepare_weights(g_mix, w_in, g_q_lat, w_uq, g_kv_lat, w_ukv, g_qn_nope, g_qn_rope,
                         g_kn_nope, g_kn_rope, conv_w, g_out_attn, g_out_conv, w_o, g_mlp, w_up,
                         w_down, g_ple, w_ple_gate, w_ple)
    tables = _rope_tables(positions)
    p_rows = p.reshape(depth, b * s, PLE_DIM)
    for layer in range(depth):
        qt, k, vt, conv = _proj_call(layer, x, tables, w)
        attn = _attn_call(qt, k, vt)
        x = _post_call(layer, x.reshape(b * s, d), attn.reshape(b * s, ATTN_WIDTH),
                       conv.reshape(b * s, CONV_WIDTH), p_rows, w).reshape(b, s, d)
    return x
```

```python
import functools

import jax
import jax.numpy as jnp
from jax import lax
from jax.experimental import pallas as pl
from jax.experimental.pallas import tpu as pltpu

D_MODEL = 1024
N_HEADS = 8
QK_NOPE = 64
QK_ROPE = 32
QK_HEAD = QK_NOPE + QK_ROPE
V_HEAD = 64
Q_LORA = 384
KV_LORA = 256
ATTN_WIDTH = N_HEADS * V_HEAD
CONV_WIDTH = D_MODEL - ATTN_WIDTH
CONV_TAPS = 3
D_FF = 4 * D_MODEL
PLE_DIM = 256
ROPE_THETA = 10000.0
EPS = 1e-6
LOG2_E = 1.4426950408889634

LANES = 128
SUBLANES = 8
HEAD_PAD = LANES
QK_WIDTH = N_HEADS * HEAD_PAD
ROPE_LO = QK_NOPE
ROPE_MID = QK_NOPE + QK_ROPE // 2
ROPE_HI = QK_NOPE + QK_ROPE
C_Q = 0
C_KV = C_Q + Q_LORA
C_KPE = C_KV + KV_LORA
C_B = C_KPE + LANES
C_C = C_B + CONV_WIDTH
C_X = C_C + CONV_WIDTH
IN_COLS_PAD = C_X + CONV_WIDTH

ROW_TILE = 512
Q_TILE = 512
KV_TILE = 512
ATTN_HEADS_PER_STEP = 4
FF_CHUNK = 1024
POST_TILE = 512
POST_PARTS = 2
VMEM_LIMIT = 56 * 1024 * 1024

_BF16 = jnp.bfloat16
_F32 = jnp.float32


def _dot(a, b):
    return jnp.dot(a, b, preferred_element_type=_F32)


def _rms(x, g, n):
    ms = jnp.sum(x * x, axis=-1, keepdims=True) * (1.0 / n)
    return x * lax.rsqrt(ms + EPS) * g


def _rope_partner(t, lane):
    return jnp.where(lane < ROPE_MID, pltpu.roll(t, LANES - QK_ROPE // 2, axis=1),
                     pltpu.roll(t, QK_ROPE // 2, axis=1))


def _rope_table_kernel(pos_ref, invf_t_ref, cos_ref, sin_ref, cos_t_ref, sin_t_ref):
    c = jnp.cos(pos_ref[0].astype(_F32) * invf_t_ref[...])
    s = jnp.sin(pos_ref[0].astype(_F32) * invf_t_ref[...])
    cos_t_ref[0] = c
    sin_t_ref[0] = s
    t = c.shape[1]
    lo, hi = (ROPE_LO, t), (LANES - ROPE_HI, t)
    cos_ref[0] = jnp.concatenate([jnp.ones(lo, _F32), c, c, jnp.ones(hi, _F32)], axis=0).T
    sin_ref[0] = jnp.concatenate([jnp.zeros(lo, _F32), -s, s, jnp.zeros(hi, _F32)], axis=0).T


def _rope_tables(positions):
    b, s = positions.shape
    half = QK_ROPE // 2
    inv_freq = 1.0 / (ROPE_THETA ** (jnp.arange(0, QK_ROPE, 2, dtype=_F32) / QK_ROPE))
    t = ROW_TILE
    invf_t = jnp.broadcast_to(inv_freq.reshape(half, 1), (half, t))
    out = pl.BlockSpec((1, t, LANES), lambda i, j: (i, j, 0))
    out_t = pl.BlockSpec((1, half, t), lambda i, j: (i, 0, j))
    return pl.pallas_call(
        _rope_table_kernel,
        out_shape=(jax.ShapeDtypeStruct((b, s, LANES), _F32),) * 2
        + (jax.ShapeDtypeStruct((b, half, s), _F32),) * 2,
        grid=(b, s // t),
        in_specs=[pl.BlockSpec((1, 1, t), lambda i, j: (i, 0, j)),
                  pl.BlockSpec((half, t), lambda i, j: (0, 0))],
        out_specs=(out, out, out_t, out_t),
        name="rope_tables",
    )(positions.reshape(b, 1, s), invf_t)


def _proj_kernel(x_ref, cos_ref, sin_ref, cos_t_ref, sin_t_ref, gmix_ref, win_ref, gq_ref,
                 wuqt_ref, gkv_ref, wuk_ref, wuvt_ref, gqn_t_ref, gkn_ref, gkpe_ref, convw_ref,
                 gconv_ref, qt_out, k_out, vt_out, conv_out, u_ref):
    t = x_ref.shape[1]
    lane = lax.broadcasted_iota(jnp.int32, (1, LANES), 1)
    cos = cos_ref[0]
    sin = sin_ref[0]
    nt = (((1,), (1,)), ((), ()))

    @pl.when(pl.program_id(1) == 0)
    def _():
        u_ref[0:SUBLANES, :] = jnp.zeros((SUBLANES, CONV_WIDTH), _F32)

    h = _rms(x_ref[0], gmix_ref[...], D_MODEL).astype(_BF16)
    q_lat = _dot(h, win_ref[:, C_Q:C_KV])
    kv_lat = _dot(h, win_ref[:, C_KV:C_KPE])
    kpe = _dot(h, win_ref[:, C_KPE:C_B])
    u_c = _dot(h, win_ref[:, C_C:C_X])

    qn = _rms(q_lat, gq_ref[...], Q_LORA).astype(_BF16)
    kvn = _rms(kv_lat, gkv_ref[...], KV_LORA).astype(_BF16)
    u_x = _dot(h, win_ref[:, C_X:IN_COLS_PAD])
    ss_pe = jnp.sum(kpe * kpe, axis=-1, keepdims=True)
    kpe_n = kpe * lax.rsqrt(ss_pe * (1.0 / QK_ROPE) + EPS) * gkpe_ref[...]
    kpe_r = kpe_n * cos + _rope_partner(kpe_n, lane) * sin
    qt = lax.dot_general(wuqt_ref[...], qn, nt, preferred_element_type=_F32)
    u = u_c * u_x
    u_ref[SUBLANES:SUBLANES + t, :] = u
    kf = _dot(kvn, wuk_ref[...])

    cos_t = cos_t_ref[0]
    sin_t = sin_t_ref[0]
    half = QK_ROPE // 2
    for hd in range(N_HEADS):
        r0 = hd * HEAD_PAD
        nope = qt[r0:r0 + ROPE_LO, :]
        rope = qt[r0 + ROPE_LO:r0 + ROPE_HI, :]
        ss_n = jnp.sum(nope * nope, axis=0, keepdims=True)
        ss_r = jnp.sum(rope * rope, axis=0, keepdims=True)
        nope = nope * lax.rsqrt(ss_n * (1.0 / QK_NOPE) + EPS) * gqn_t_ref[0:ROPE_LO, :]
        rope = rope * lax.rsqrt(ss_r * (1.0 / QK_ROPE) + EPS) * gqn_t_ref[ROPE_LO:ROPE_HI, :]
        x1 = rope[0:half, :]
        x2 = rope[half:QK_ROPE, :]
        qt_out[0, r0:r0 + ROPE_LO, :] = nope.astype(_BF16)
        qt_out[0, r0 + ROPE_LO:r0 + ROPE_MID, :] = (x1 * cos_t - x2 * sin_t).astype(_BF16)
        qt_out[0, r0 + ROPE_MID:r0 + ROPE_HI, :] = (x2 * cos_t + x1 * sin_t).astype(_BF16)
        qt_out[0, r0 + ROPE_HI:r0 + HEAD_PAD, :] = jnp.zeros((HEAD_PAD - ROPE_HI, t), _BF16)

    gate_b = _dot(h, win_ref[:, C_B:C_C])
    vt_out[0, 0] = lax.dot_general(wuvt_ref[...], kvn, nt,
                                   preferred_element_type=_F32).astype(_BF16)

    rows = t // N_HEADS
    for hd in range(N_HEADS):
        th = kf[:, hd * HEAD_PAD:(hd + 1) * HEAD_PAD]
        ss = jnp.sum(th * th, axis=-1, keepdims=True)
        kh = th * lax.rsqrt(ss * (1.0 / QK_NOPE) + EPS) * gkn_ref[...] + kpe_r
        k_out[0, :, hd * HEAD_PAD:(hd + 1) * HEAD_PAD] = kh.astype(_BF16)

        a = hd * rows
        y = convw_ref[0:1, :] * u[a:a + rows, :]
        for j in range(1, CONV_TAPS):
            y = y + convw_ref[j:j + 1, :] * u_ref[SUBLANES - j + a:SUBLANES - j + a + rows, :]
        conv_out[0, a:a + rows, :] = _rms(gate_b[a:a + rows, :] * y, gconv_ref[...],
                                          CONV_WIDTH).astype(_BF16)
    u_ref[0:SUBLANES, :] = u[t - SUBLANES:t, :]


def _proj_call(layer, x, tables, w):
    b, s, _ = x.shape
    t = ROW_TILE
    cos, sin, cos_t, sin_t = tables

    def row(width):
        return pl.BlockSpec((1, t, width), lambda i, j: (i, j, 0))

    def col(height):
        return pl.BlockSpec((1, height, t), lambda i, j: (i, 0, j))

    def per_layer(arr):
        return pl.BlockSpec((None,) + arr.shape[1:], lambda i, j: (layer,) + (0,) * (arr.ndim - 1))

    params = [w["g_mix"], w["w_in"], w["g_q_lat"], w["w_uqt"], w["g_kv_lat"], w["w_uk"],
              w["w_uvt"], w["g_qn_t"], w["g_kn"], w["g_kpe"], w["conv_w"], w["g_out_conv"]]
    assert t == KV_TILE
    return pl.pallas_call(
        _proj_kernel,
        out_shape=(jax.ShapeDtypeStruct((b, QK_WIDTH, s), _BF16),
                   jax.ShapeDtypeStruct((b, s, QK_WIDTH), _BF16),
                   jax.ShapeDtypeStruct((b, s // t, ATTN_WIDTH, t), _BF16),
                   jax.ShapeDtypeStruct((b, s, CONV_WIDTH), _BF16)),
        grid=(b, s // t),
        in_specs=[row(D_MODEL), row(LANES), row(LANES), col(QK_ROPE // 2), col(QK_ROPE // 2)]
                 + [per_layer(a) for a in params],
        out_specs=(col(QK_WIDTH), row(QK_WIDTH),
                   pl.BlockSpec((1, 1, ATTN_WIDTH, t), lambda i, j: (i, j, 0, 0)),
                   row(CONV_WIDTH)),
        scratch_shapes=[pltpu.VMEM((SUBLANES + t, CONV_WIDTH), _F32)],
        compiler_params=pltpu.CompilerParams(
            dimension_semantics=("arbitrary", "arbitrary"), vmem_limit_bytes=VMEM_LIMIT),
        name="proj",
    )(x, cos, sin, cos_t, sin_t, *params)


def _attn_kernel(qt_ref, qt_next_ref, k_ref, vt_ref, o_ref, m_ref, acc_ref, s0_ref, s1_ref,
                 xa_ref, xb_ref):
    tq = qt_ref.shape[2]
    tk = KV_TILE
    qi = pl.program_id(2)
    heads = qt_ref.shape[1] // HEAD_PAD

    m_ref[...] = jnp.full(m_ref.shape, -jnp.inf, _F32)
    acc_ref[...] = jnp.zeros(acc_ref.shape, _F32)
    ones = jnp.ones((V_HEAD, tk), _BF16)

    def load_values(t):
        return [vt_ref[0, t, hh * V_HEAD:(hh + 1) * V_HEAD, :] for hh in range(heads)]

    def load_keys(t):
        start = t * tk if isinstance(t, int) else pl.multiple_of(t * tk, tk)
        return [k_ref[0, pl.ds(start, tk), hh * HEAD_PAD:(hh + 1) * HEAD_PAD]
                for hh in range(heads)]

    def scores(hh, kbs, s_ref, q_ref):
        qt = q_ref[0, hh * HEAD_PAD:(hh + 1) * HEAD_PAD, :]
        st = _dot(kbs[hh], qt)
        s_ref[hh, 0:tk, :] = st
        s_ref[hh, tk:tk + 1, :] = jnp.max(st, axis=0, keepdims=True)

    def accumulate(hh, st, block_max, vt, q0, q1):
        m_old = m_ref[hh, :, q0:q1]
        m_new = jnp.maximum(m_old, block_max)
        alpha = jnp.exp2(m_old - m_new)
        pt = jnp.exp2(st - m_new).astype(_BF16)
        acc_ref[hh, :, q0:q1] = alpha * acc_ref[hh, :, q0:q1] + _dot(vt, pt)
        m_ref[hh, :, q0:q1] = m_new

    def causal(st):
        key = lax.broadcasted_iota(jnp.int32, st.shape, 0)
        qry = lax.broadcasted_iota(jnp.int32, st.shape, 1)
        return jnp.where(key <= qry, st, -jnp.inf)

    def softmax_values(hh, s_ref, vts, masked):
        vt = jnp.concatenate([vts[hh], ones], axis=0)
        if not masked:
            accumulate(hh, s_ref[hh, 0:tk, :], s_ref[hh, tk:tk + 1, :], vt, 0, tq)
            return
        hk = tk // 2
        st = causal(s_ref[hh, 0:hk, :])
        accumulate(hh, st, jnp.max(st, axis=0, keepdims=True), vt[:, 0:hk], 0, tq)
        st = causal(s_ref[hh, hk:tk, hk:tq])
        accumulate(hh, st, jnp.max(st, axis=0, keepdims=True), vt[:, hk:tk], hk, tq)

    def score_stage(kbs, s_ref, q_ref=qt_ref):
        for hh in range(heads):
            scores(hh, kbs, s_ref, q_ref)

    def fused_step(t_cur, s_cur, masked, t_next, s_next, q_next):
        vts = load_values(t_cur)
        kbs = load_keys(t_next)
        for hh in range(heads):
            scores(hh, kbs, s_next, q_next)
            softmax_values(hh, s_cur, vts, masked)

    def pipelined_step(t, s_cur, s_next):
        fused_step(t, s_cur, False, t + 1, s_next, qt_ref)

    def last_step(t, s_cur, x_out):
        fused_step(t, s_cur, True, 0, x_out, qt_next_ref)
        outs = []
        for hh in range(heads):
            acc = acc_ref[hh]
            outs.append(acc[0:V_HEAD, :] / acc[V_HEAD:V_HEAD + 1, :])
        o_ref[0] = jnp.concatenate(outs, axis=0).T.astype(o_ref.dtype)

    def middle_steps(x_in):
        pipelined_step(0, x_in, s0_ref)

        def pair(p, carry):
            pipelined_step(2 * p + 1, s0_ref, s1_ref)
            pipelined_step(2 * p + 2, s1_ref, s0_ref)
            return carry

        lax.fori_loop(0, lax.shift_right_logical(qi - 1, 1), pair, 0)

    odd = (qi & 1) == 1

    @pl.when(qi == 0)
    def _():
        score_stage(load_keys(0), xa_ref)
        last_step(0, xa_ref, xb_ref)

    @pl.when(jnp.logical_and(qi > 0, jnp.logical_not(odd)))
    def _():
        middle_steps(xa_ref)
        pipelined_step(qi - 1, s0_ref, s1_ref)
        last_step(qi, s1_ref, xb_ref)

    @pl.when(odd)
    def _():
        middle_steps(xb_ref)
        last_step(qi, s0_ref, xa_ref)


def _attn_call(qt, k, vt):
    b, s, _ = k.shape
    heads = ATTN_HEADS_PER_STEP
    assert (heads * V_HEAD) % LANES == 0 and N_HEADS % heads == 0
    groups = N_HEADS // heads
    tq = Q_TILE
    assert Q_TILE == KV_TILE
    return pl.pallas_call(
        _attn_kernel,
        out_shape=jax.ShapeDtypeStruct((b, s, ATTN_WIDTH), _BF16),
        grid=(b, groups, s // tq),
        in_specs=[pl.BlockSpec((1, heads * HEAD_PAD, tq), lambda i, g, j: (i, g, j)),
                  pl.BlockSpec((1, heads * HEAD_PAD, tq),
                               lambda i, g, j: (i, g, jnp.minimum(j + 1, s // tq - 1))),
                  pl.BlockSpec((1, s, heads * HEAD_PAD), lambda i, g, j: (i, 0, g)),
                  pl.BlockSpec((1, s // KV_TILE, heads * V_HEAD, KV_TILE),
                               lambda i, g, j: (i, 0, g, 0))],
        out_specs=pl.BlockSpec((1, tq, heads * V_HEAD), lambda i, g, j: (i, j, g)),
        scratch_shapes=[pltpu.VMEM((heads, 1, tq), _F32),
                        pltpu.VMEM((heads, 2 * V_HEAD, tq), _F32),
                        ] + [pltpu.VMEM((heads, KV_TILE + SUBLANES, tq), _F32)] * 4,
        compiler_params=pltpu.CompilerParams(
            dimension_semantics=("arbitrary", "arbitrary", "arbitrary"),
            vmem_limit_bytes=VMEM_LIMIT),
        name="attn",
    )(qt, qt, k, vt)


def _post_kernel(x_ref, attn_ref, conv_ref, p_ref, gattn_ref, wo_ref, gmlp_ref, wup_ref,
                 wdown_ref, gple_ref, wgate_ref, wple_ref, o_ref):
    t = x_ref.shape[0]
    rows = t // POST_PARTS
    halves = [slice(i * rows, (i + 1) * rows) for i in range(POST_PARTS)]

    def out_proj(r):
        an = _rms(attn_ref[r, :].astype(_F32), gattn_ref[...], ATTN_WIDTH).astype(_BF16)
        return (x_ref[r, :] + _dot(an, wo_ref[0:ATTN_WIDTH, :])
                + _dot(conv_ref[r, :], wo_ref[ATTN_WIDTH:, :]))

    def mlp(x):
        h2 = _rms(x, gmlp_ref[...], D_MODEL).astype(_BF16)
        acc = None
        for c in range(D_FF // FF_CHUNK):
            up = _dot(h2, wup_ref[:, c * FF_CHUNK:(c + 1) * FF_CHUNK])
            act = jnp.square(jnp.maximum(up, 0.0)).astype(_BF16)
            part = _dot(act, wdown_ref[c * FF_CHUNK:(c + 1) * FF_CHUNK, :])
            acc = part if acc is None else acc + part
        return x + acc

    def embed(r):
        return _dot(p_ref[r, :].astype(_BF16), wple_ref[...])

    def gated_add(r, x, emb):
        gate = jax.nn.sigmoid(_dot(_rms(x, gple_ref[...], D_MODEL).astype(_BF16), wgate_ref[...]))
        o_ref[r, :] = x + gate * emb

    embs, xs = [], []
    for r in halves:
        embs.append(embed(r))
        xs.append(out_proj(r))
    xs = [mlp(x) for x in xs]
    for r, x, emb in zip(halves, xs, embs):
        gated_add(r, x, emb)


def _post_call(layer, x, attn, conv, p, w):
    n = x.shape[0]
    t = POST_TILE

    def row(width):
        return pl.BlockSpec((t, width), lambda i: (i, 0))

    def per_layer(arr):
        return pl.BlockSpec((None,) + arr.shape[1:], lambda i: (layer,) + (0,) * (arr.ndim - 1),
                            pipeline_mode=pl.Buffered(1))

    params = [w["g_out_attn"], w["w_o"], w["g_mlp"], w["w_up"], w["w_down"], w["g_ple"],
              w["w_ple_gate"], w["w_ple"]]
    return pl.pallas_call(
        _post_kernel,
        out_shape=jax.ShapeDtypeStruct((n, D_MODEL), _F32),
        grid=(n // t,),
        in_specs=[row(D_MODEL), row(ATTN_WIDTH), row(CONV_WIDTH),
                  pl.BlockSpec((None, t, PLE_DIM), lambda i: (layer, i, 0))]
                 + [per_layer(a) for a in params],
        out_specs=row(D_MODEL),
        compiler_params=pltpu.CompilerParams(
            dimension_semantics=("arbitrary",), vmem_limit_bytes=VMEM_LIMIT),
        name="post",
    )(x, attn, conv, p, *params)


def _prepare_weights(g_mix, w_in, g_q_lat, w_uq, g_kv_lat, w_ukv, g_qn_nope, g_qn_rope,
                     g_kn_nope, g_kn_rope, conv_w, g_out_attn, g_out_conv, w_o, g_mlp, w_up,
                     w_down, g_ple, w_ple_gate, w_ple):
    depth = w_in.shape[0]

    def gain(g):
        return g.reshape(depth, 1, -1).astype(_F32)

    def zeros(*shape):
        return jnp.zeros((depth,) + shape, _F32)

    o_kpe = Q_LORA + KV_LORA
    w_in = w_in.astype(_BF16)
    zcols = jnp.zeros((depth, D_MODEL, LANES), _BF16)
    w_in_p = jnp.concatenate(
        [w_in[..., :o_kpe], zcols[..., :ROPE_LO], w_in[..., o_kpe:o_kpe + QK_ROPE],
         zcols[..., :LANES - ROPE_HI], w_in[..., o_kpe + QK_ROPE:]], axis=-1)
    w_uq_p = jnp.pad(w_uq.astype(_BF16).reshape(depth, Q_LORA, N_HEADS, QK_HEAD),
                     ((0, 0), (0, 0), (0, 0), (0, HEAD_PAD - QK_HEAD))).reshape(depth, Q_LORA, QK_WIDTH)
    w_ukv_h = w_ukv.astype(_BF16).reshape(depth, KV_LORA, N_HEADS, QK_NOPE + V_HEAD)
    w_uk_p = jnp.pad(w_ukv_h[..., :QK_NOPE],
                     ((0, 0), (0, 0), (0, 0), (0, HEAD_PAD - QK_NOPE))).reshape(depth, KV_LORA, QK_WIDTH)
    w_uvt = jnp.swapaxes(w_ukv_h[..., QK_NOPE:].reshape(depth, KV_LORA, ATTN_WIDTH), 1, 2)
    q_scale = QK_HEAD ** -0.5 * LOG2_E
    g_qn = jnp.concatenate([g_qn_nope, g_qn_rope, zeros(LANES - ROPE_HI)], axis=-1) * q_scale
    g_qn_t = jnp.broadcast_to(g_qn[:, :, None], (depth, HEAD_PAD, ROW_TILE)).astype(_F32)
    return {
        "g_mix": gain(g_mix), "w_in": w_in_p,
        "g_q_lat": gain(g_q_lat), "w_uqt": jnp.swapaxes(w_uq_p, 1, 2),
        "g_kv_lat": gain(g_kv_lat), "w_uk": w_uk_p, "w_uvt": w_uvt,
        "g_qn_t": g_qn_t,
        "g_kn": gain(jnp.concatenate([g_kn_nope, zeros(LANES - QK_NOPE)], axis=-1)),
        "g_kpe": gain(jnp.concatenate([zeros(ROPE_LO), g_kn_rope, zeros(LANES - ROPE_HI)], axis=-1)),
        "conv_w": conv_w.astype(_F32), "g_out_conv": gain(g_out_conv),
        "g_out_attn": gain(g_out_attn), "w_o": w_o.astype(_BF16), "g_mlp": gain(g_mlp),
        "w_up": w_up.astype(_BF16), "w_down": w_down.astype(_BF16), "g_ple": gain(g_ple),
        "w_ple_gate": w_ple_gate.astype(_BF16), "w_ple": w_ple.astype(_BF16),
    }


def kernel(x, p, positions, g_mix, w_in, g_q_lat, w_uq, g_kv_lat, w_ukv, g_qn_nope, g_qn_rope,
           g_kn_nope, g_kn_rope, conv_w, g_out_attn, g_out_conv, w_o, g_mlp, w_up, w_down, g_ple,
           w_ple_gate, w_ple):
    b, s, d = x.shape
    depth = w_in.shape[0]
    assert d == D_MODEL and s % ROW_TILE == 0 and s % Q_TILE == 0
    w = _prepare_weights(g_mix, w_in, g_q_lat, w_uq, g_kv_lat, w_ukv, g_qn_nope, g_qn_rope,
                         g_kn_nope, g_kn_rope, conv_w, g_out_attn, g_out_conv, w_o, g_mlp, w_up,
                         w_down, g_ple, w_ple_gate, w_ple)
    tables = _rope_tables(positions)
    p_rows = p.reshape(depth, b * s, PLE_DIM)
    for layer in range(depth):
        qt, k, vt, conv = _proj_call(layer, x, tables, w)
        attn = _attn_call(qt, k, vt)
        x = _post_call(layer, x.reshape(b * s, d), attn.reshape(b * s, ATTN_WIDTH),
                       conv.reshape(b * s, CONV_WIDTH), p_rows, w).reshape(b, s, d)
    return x
```

```python
import functools

import jax
import jax.numpy as jnp
from jax import lax
from jax.experimental import pallas as pl
from jax.experimental.pallas import tpu as pltpu

D_MODEL = 1024
N_HEADS = 8
QK_NOPE = 64
QK_ROPE = 32
QK_HEAD = QK_NOPE + QK_ROPE
V_HEAD = 64
Q_LORA = 384
KV_LORA = 256
ATTN_WIDTH = N_HEADS * V_HEAD
CONV_WIDTH = D_MODEL - ATTN_WIDTH
CONV_TAPS = 3
D_FF = 4 * D_MODEL
PLE_DIM = 256
ROPE_THETA = 10000.0
EPS = 1e-6
LOG2_E = 1.4426950408889634

LANES = 128
SUBLANES = 8
HEAD_PAD = LANES
QK_WIDTH = N_HEADS * HEAD_PAD
ROPE_LO = QK_NOPE
ROPE_MID = QK_NOPE + QK_ROPE // 2
ROPE_HI = QK_NOPE + QK_ROPE
C_Q = 0
C_KV = C_Q + Q_LORA
C_KPE = C_KV + KV_LORA
LAT_COLS = C_KPE + LANES
C_B = 0
C_C = C_B + CONV_WIDTH
C_X = C_C + CONV_WIDTH
CONV_COLS = C_X + CONV_WIDTH

ROW_TILE = 512
Q_TILE = 512
KV_TILE = 512
ATTN_HEADS_PER_STEP = 4
FF_CHUNK = 1024
POST_TILE = 512
POST_PARTS = 2
VMEM_LIMIT = 56 * 1024 * 1024

_BF16 = jnp.bfloat16
_F32 = jnp.float32


def _dot(a, b):
    return jnp.dot(a, b, preferred_element_type=_F32)


def _rms(x, g, n):
    ms = jnp.sum(x * x, axis=-1, keepdims=True) * (1.0 / n)
    return x * lax.rsqrt(ms + EPS) * g


def _rope_partner(t, lane):
    return jnp.where(lane < ROPE_MID, pltpu.roll(t, LANES - QK_ROPE // 2, axis=1),
                     pltpu.roll(t, QK_ROPE // 2, axis=1))


def _rope_table_kernel(pos_ref, invf_t_ref, cos_ref, sin_ref, cos_t_ref, sin_t_ref):
    c = jnp.cos(pos_ref[0].astype(_F32) * invf_t_ref[...])
    s = jnp.sin(pos_ref[0].astype(_F32) * invf_t_ref[...])
    cos_t_ref[0] = c
    sin_t_ref[0] = s
    t = c.shape[1]
    lo, hi = (ROPE_LO, t), (LANES - ROPE_HI, t)
    cos_ref[0] = jnp.concatenate([jnp.ones(lo, _F32), c, c, jnp.ones(hi, _F32)], axis=0).T
    sin_ref[0] = jnp.concatenate([jnp.zeros(lo, _F32), -s, s, jnp.zeros(hi, _F32)], axis=0).T


def _rope_tables(positions):
    b, s = positions.shape
    half = QK_ROPE // 2
    inv_freq = 1.0 / (ROPE_THETA ** (jnp.arange(0, QK_ROPE, 2, dtype=_F32) / QK_ROPE))
    t = ROW_TILE
    invf_t = jnp.broadcast_to(inv_freq.reshape(half, 1), (half, t))
    out = pl.BlockSpec((1, t, LANES), lambda i, j: (i, j, 0))
    out_t = pl.BlockSpec((1, half, t), lambda i, j: (i, 0, j))
    return pl.pallas_call(
        _rope_table_kernel,
        out_shape=(jax.ShapeDtypeStruct((b, s, LANES), _F32),) * 2
        + (jax.ShapeDtypeStruct((b, half, s), _F32),) * 2,
        grid=(b, s // t),
        in_specs=[pl.BlockSpec((1, 1, t), lambda i, j: (i, 0, j)),
                  pl.BlockSpec((half, t), lambda i, j: (0, 0))],
        out_specs=(out, out, out_t, out_t),
        name="rope_tables",
    )(positions.reshape(b, 1, s), invf_t)


def _proj_kernel(x_ref, cos_ref, sin_ref, cos_t_ref, sin_t_ref, gmix_ref, wlat_ref, wconv_ref, gq_ref,
                 wuqt_ref, gkv_ref, wuk_ref, wuvt_ref, gqn_t_ref, gkn_ref, gkpe_ref, convw_ref,
                 gconv_ref, qt_out, k_out, vt_out, conv_out, u_ref):
    t = x_ref.shape[1]
    lane = lax.broadcasted_iota(jnp.int32, (1, LANES), 1)
    cos = cos_ref[0]
    sin = sin_ref[0]
    nt = (((1,), (1,)), ((), ()))

    @pl.when(pl.program_id(1) == 0)
    def _():
        u_ref[0:SUBLANES, :] = jnp.zeros((SUBLANES, CONV_WIDTH), _F32)

    h = _rms(x_ref[0], gmix_ref[...], D_MODEL).astype(_BF16)
    lat = _dot(h, wlat_ref[...])
    q_lat = lat[:, C_Q:C_KV]
    kv_lat = lat[:, C_KV:C_KPE]
    kpe = lat[:, C_KPE:LAT_COLS]
    u_c = _dot(h, wconv_ref[:, C_C:C_X])

    qn = _rms(q_lat, gq_ref[...], Q_LORA).astype(_BF16)
    kvn = _rms(kv_lat, gkv_ref[...], KV_LORA).astype(_BF16)
    u_x = _dot(h, wconv_ref[:, C_X:CONV_COLS])
    ss_pe = jnp.sum(kpe * kpe, axis=-1, keepdims=True)
    kpe_n = kpe * lax.rsqrt(ss_pe * (1.0 / QK_ROPE) + EPS) * gkpe_ref[...]
    kpe_r = kpe_n * cos + _rope_partner(kpe_n, lane) * sin
    qt = lax.dot_general(wuqt_ref[...], qn, nt, preferred_element_type=_F32)
    u = u_c * u_x
    u_ref[SUBLANES:SUBLANES + t, :] = u
    kf = _dot(kvn, wuk_ref[...])

    cos_t = cos_t_ref[0]
    sin_t = sin_t_ref[0]
    half = QK_ROPE // 2
    for hd in range(N_HEADS):
        r0 = hd * HEAD_PAD
        nope = qt[r0:r0 + ROPE_LO, :]
        rope = qt[r0 + ROPE_LO:r0 + ROPE_HI, :]
        ss_n = jnp.sum(nope * nope, axis=0, keepdims=True)
        ss_r = jnp.sum(rope * rope, axis=0, keepdims=True)
        nope = nope * lax.rsqrt(ss_n * (1.0 / QK_NOPE) + EPS) * gqn_t_ref[0:ROPE_LO, :]
        rope = rope * lax.rsqrt(ss_r * (1.0 / QK_ROPE) + EPS) * gqn_t_ref[ROPE_LO:ROPE_HI, :]
        x1 = rope[0:half, :]
        x2 = rope[half:QK_ROPE, :]
        qt_out[0, r0:r0 + ROPE_LO, :] = nope.astype(_BF16)
        qt_out[0, r0 + ROPE_LO:r0 + ROPE_MID, :] = (x1 * cos_t - x2 * sin_t).astype(_BF16)
        qt_out[0, r0 + ROPE_MID:r0 + ROPE_HI, :] = (x2 * cos_t + x1 * sin_t).astype(_BF16)
        qt_out[0, r0 + ROPE_HI:r0 + HEAD_PAD, :] = jnp.zeros((HEAD_PAD - ROPE_HI, t), _BF16)

    gate_b = _dot(h, wconv_ref[:, C_B:C_C])
    vt_out[0, 0] = lax.dot_general(wuvt_ref[...], kvn, nt,
                                   preferred_element_type=_F32).astype(_BF16)

    rows = t // N_HEADS
    for hd in range(N_HEADS):
        th = kf[:, hd * HEAD_PAD:(hd + 1) * HEAD_PAD]
        ss = jnp.sum(th * th, axis=-1, keepdims=True)
        kh = th * lax.rsqrt(ss * (1.0 / QK_NOPE) + EPS) * gkn_ref[...] + kpe_r
        k_out[0, :, hd * HEAD_PAD:(hd + 1) * HEAD_PAD] = kh.astype(_BF16)

        a = hd * rows
        y = convw_ref[0:1, :] * u[a:a + rows, :]
        for j in range(1, CONV_TAPS):
            y = y + convw_ref[j:j + 1, :] * u_ref[SUBLANES - j + a:SUBLANES - j + a + rows, :]
        conv_out[0, a:a + rows, :] = _rms(gate_b[a:a + rows, :] * y, gconv_ref[...],
                                          CONV_WIDTH).astype(_BF16)
    u_ref[0:SUBLANES, :] = u[t - SUBLANES:t, :]


def _proj_call(layer, x, tables, w):
    b, s, _ = x.shape
    t = ROW_TILE
    cos, sin, cos_t, sin_t = tables

    def row(width):
        return pl.BlockSpec((1, t, width), lambda i, j: (i, j, 0))

    def col(height):
        return pl.BlockSpec((1, height, t), lambda i, j: (i, 0, j))

    def per_layer(arr):
        return pl.BlockSpec((None,) + arr.shape[1:], lambda i, j: (layer,) + (0,) * (arr.ndim - 1))

    params = [w["g_mix"], w["w_lat"], w["w_conv"], w["g_q_lat"], w["w_uqt"], w["g_kv_lat"], w["w_uk"],
              w["w_uvt"], w["g_qn_t"], w["g_kn"], w["g_kpe"], w["conv_w"], w["g_out_conv"]]
    assert t == KV_TILE
    return pl.pallas_call(
        _proj_kernel,
        out_shape=(jax.ShapeDtypeStruct((b, QK_WIDTH, s), _BF16),
                   jax.ShapeDtypeStruct((b, s, QK_WIDTH), _BF16),
                   jax.ShapeDtypeStruct((b, s // t, ATTN_WIDTH, t), _BF16),
                   jax.ShapeDtypeStruct((b, s, CONV_WIDTH), _BF16)),
        grid=(b, s // t),
        in_specs=[row(D_MODEL), row(LANES), row(LANES), col(QK_ROPE // 2), col(QK_ROPE // 2)]
                 + [per_layer(a) for a in params],
        out_specs=(col(QK_WIDTH), row(QK_WIDTH),
                   pl.BlockSpec((1, 1, ATTN_WIDTH, t), lambda i, j: (i, j, 0, 0)),
                   row(CONV_WIDTH)),
        scratch_shapes=[pltpu.VMEM((SUBLANES + t, CONV_WIDTH), _F32)],
        compiler_params=pltpu.CompilerParams(
            dimension_semantics=("arbitrary", "arbitrary"), vmem_limit_bytes=VMEM_LIMIT),
        name="proj",
    )(x, cos, sin, cos_t, sin_t, *params)


def _attn_kernel(qt_ref, qt_next_ref, k_ref, vt_ref, o_ref, m_ref, acc_ref, s0_ref, s1_ref,
                 xa_ref, xb_ref):
    tq = qt_ref.shape[2]
    tk = KV_TILE
    qi = pl.program_id(2)
    heads = qt_ref.shape[1] // HEAD_PAD

    m_ref[...] = jnp.full(m_ref.shape, -jnp.inf, _F32)
    acc_ref[...] = jnp.zeros(acc_ref.shape, _F32)
    ones = jnp.ones((V_HEAD, tk), _BF16)

    def load_values(t):
        return [vt_ref[0, t, hh * V_HEAD:(hh + 1) * V_HEAD, :] for hh in range(heads)]

    def load_keys(t):
        start = t * tk if isinstance(t, int) else pl.multiple_of(t * tk, tk)
        return [k_ref[0, pl.ds(start, tk), hh * HEAD_PAD:(hh + 1) * HEAD_PAD]
                for hh in range(heads)]

    def scores(hh, kbs, s_ref, q_ref):
        qt = q_ref[0, hh * HEAD_PAD:(hh + 1) * HEAD_PAD, :]
        st = _dot(kbs[hh], qt)
        s_ref[hh, 0:tk, :] = st
        s_ref[hh, tk:tk + 1, :] = jnp.max(st, axis=0, keepdims=True)

    def accumulate(hh, st, block_max, vt, q0, q1):
        m_old = m_ref[hh, :, q0:q1]
        m_new = jnp.maximum(m_old, block_max)
        alpha = jnp.exp2(m_old - m_new)
        pt = jnp.exp2(st - m_new).astype(_BF16)
        acc_ref[hh, :, q0:q1] = alpha * acc_ref[hh, :, q0:q1] + _dot(vt, pt)
        m_ref[hh, :, q0:q1] = m_new

    def causal(st):
        key = lax.broadcasted_iota(jnp.int32, st.shape, 0)
        qry = lax.broadcasted_iota(jnp.int32, st.shape, 1)
        return jnp.where(key <= qry, st, -jnp.inf)

    def softmax_values(hh, s_ref, vts, masked):
        vt = jnp.concatenate([vts[hh], ones], axis=0)
        if not masked:
            accumulate(hh, s_ref[hh, 0:tk, :], s_ref[hh, tk:tk + 1, :], vt, 0, tq)
            return
        hk = tk // 2
        st = causal(s_ref[hh, 0:hk, :])
        accumulate(hh, st, jnp.max(st, axis=0, keepdims=True), vt[:, 0:hk], 0, tq)
        st = causal(s_ref[hh, hk:tk, hk:tq])
        accumulate(hh, st, jnp.max(st, axis=0, keepdims=True), vt[:, hk:tk], hk, tq)

    def score_stage(kbs, s_ref, q_ref=qt_ref):
        for hh in range(heads):
            scores(hh, kbs, s_ref, q_ref)

    def fused_step(t_cur, s_cur, masked, t_next, s_next, q_next):
        vts = load_values(t_cur)
        kbs = load_keys(t_next)
        for hh in range(heads):
            scores(hh, kbs, s_next, q_next)
            softmax_values(hh, s_cur, vts, masked)

    def pipelined_step(t, s_cur, s_next):
        fused_step(t, s_cur, False, t + 1, s_next, qt_ref)

    def last_step(t, s_cur, x_out):
        fused_step(t, s_cur, True, 0, x_out, qt_next_ref)
        outs = []
        for hh in range(heads):
            acc = acc_ref[hh]
            outs.append(acc[0:V_HEAD, :] / acc[V_HEAD:V_HEAD + 1, :])
        o_ref[0] = jnp.concatenate(outs, axis=0).T.astype(o_ref.dtype)

    def middle_steps(x_in):
        pipelined_step(0, x_in, s0_ref)

        def pair(p, carry):
            pipelined_step(2 * p + 1, s0_ref, s1_ref)
            pipelined_step(2 * p + 2, s1_ref, s0_ref)
            return carry

        lax.fori_loop(0, lax.shift_right_logical(qi - 1, 1), pair, 0)

    odd = (qi & 1) == 1

    @pl.when(qi == 0)
    def _():
        score_stage(load_keys(0), xa_ref)
        last_step(0, xa_ref, xb_ref)

    @pl.when(jnp.logical_and(qi > 0, jnp.logical_not(odd)))
    def _():
        middle_steps(xa_ref)
        pipelined_step(qi - 1, s0_ref, s1_ref)
        last_step(qi, s1_ref, xb_ref)

    @pl.when(odd)
    def _():
        middle_steps(xb_ref)
        last_step(qi, s0_ref, xa_ref)


def _attn_call(qt, k, vt):
    b, s, _ = k.shape
    heads = ATTN_HEADS_PER_STEP
    assert (heads * V_HEAD) % LANES == 0 and N_HEADS % heads == 0
    groups = N_HEADS // heads
    tq = Q_TILE
    assert Q_TILE == KV_TILE
    return pl.pallas_call(
        _attn_kernel,
        out_shape=jax.ShapeDtypeStruct((b, s, ATTN_WIDTH), _BF16),
        grid=(b, groups, s // tq),
        in_specs=[pl.BlockSpec((1, heads * HEAD_PAD, tq), lambda i, g, j: (i, g, j)),
                  pl.BlockSpec((1, heads * HEAD_PAD, tq),
                               lambda i, g, j: (i, g, jnp.minimum(j + 1, s // tq - 1))),
                  pl.BlockSpec((1, s, heads * HEAD_PAD), lambda i, g, j: (i, 0, g)),
                  pl.BlockSpec((1, s // KV_TILE, heads * V_HEAD, KV_TILE),
                               lambda i, g, j: (i, 0, g, 0))],
        out_specs=pl.BlockSpec((1, tq, heads * V_HEAD), lambda i, g, j: (i, j, g)),
        scratch_shapes=[pltpu.VMEM((heads, 1, tq), _F32),
                        pltpu.VMEM((heads, 2 * V_HEAD, tq), _F32),
                        ] + [pltpu.VMEM((heads, KV_TILE + SUBLANES, tq), _F32)] * 4,
        compiler_params=pltpu.CompilerParams(
            dimension_semantics=("arbitrary", "arbitrary", "arbitrary"),
            vmem_limit_bytes=VMEM_LIMIT),
        name="attn",
    )(qt, qt, k, vt)


def _post_kernel(x_ref, attn_ref, conv_ref, p_ref, gattn_ref, wo_ref, gmlp_ref, wup_ref,
                 wdown_ref, gple_ref, wgate_ref, wple_ref, o_ref):
    t = x_ref.shape[0]
    rows = t // POST_PARTS
    halves = [slice(i * rows, (i + 1) * rows) for i in range(POST_PARTS)]

    def out_proj(r):
        an = _rms(attn_ref[r, :].astype(_F32), gattn_ref[...], ATTN_WIDTH).astype(_BF16)
        return (x_ref[r, :] + _dot(an, wo_ref[0:ATTN_WIDTH, :])
                + _dot(conv_ref[r, :], wo_ref[ATTN_WIDTH:, :]))

    def mlp(x):
        h2 = _rms(x, gmlp_ref[...], D_MODEL).astype(_BF16)
        acc = None
        for c in range(D_FF // FF_CHUNK):
            up = _dot(h2, wup_ref[:, c * FF_CHUNK:(c + 1) * FF_CHUNK])
            act = jnp.square(jnp.maximum(up, 0.0)).astype(_BF16)
            part = _dot(act, wdown_ref[c * FF_CHUNK:(c + 1) * FF_CHUNK, :])
            acc = part if acc is None else acc + part
        return x + acc

    def embed(r):
        return _dot(p_ref[r, :].astype(_BF16), wple_ref[...])

    def gated_add(r, x, emb):
        gate = jax.nn.sigmoid(_dot(_rms(x, gple_ref[...], D_MODEL).astype(_BF16), wgate_ref[...]))
        o_ref[r, :] = x + gate * emb

    embs, xs = [], []
    for r in halves:
        embs.append(embed(r))
        xs.append(out_proj(r))
    xs = [mlp(x) for x in xs]
    for r, x, emb in zip(halves, xs, embs):
        gated_add(r, x, emb)


def _post_call(layer, x, attn, conv, p, w):
    n = x.shape[0]
    t = POST_TILE

    def row(width):
        return pl.BlockSpec((t, width), lambda i: (i, 0))

    def per_layer(arr):
        return pl.BlockSpec((None,) + arr.shape[1:], lambda i: (layer,) + (0,) * (arr.ndim - 1),
                            pipeline_mode=pl.Buffered(1))

    params = [w["g_out_attn"], w["w_o"], w["g_mlp"], w["w_up"], w["w_down"], w["g_ple"],
              w["w_ple_gate"], w["w_ple"]]
    return pl.pallas_call(
        _post_kernel,
        out_shape=jax.ShapeDtypeStruct((n, D_MODEL), _F32),
        grid=(n // t,),
        in_specs=[row(D_MODEL), row(ATTN_WIDTH), row(CONV_WIDTH),
                  pl.BlockSpec((None, t, PLE_DIM), lambda i: (layer, i, 0))]
                 + [per_layer(a) for a in params],
        out_specs=row(D_MODEL),
        compiler_params=pltpu.CompilerParams(
            dimension_semantics=("arbitrary",), vmem_limit_bytes=VMEM_LIMIT),
        name="post",
    )(x, attn, conv, p, *params)


def _prepare_weights(g_mix, w_in, g_q_lat, w_uq, g_kv_lat, w_ukv, g_qn_nope, g_qn_rope,
                     g_kn_nope, g_kn_rope, conv_w, g_out_attn, g_out_conv, w_o, g_mlp, w_up,
                     w_down, g_ple, w_ple_gate, w_ple):
    depth = w_in.shape[0]

    def gain(g):
        return g.reshape(depth, 1, -1).astype(_F32)

    def zeros(*shape):
        return jnp.zeros((depth,) + shape, _F32)

    o_kpe = Q_LORA + KV_LORA
    w_in = w_in.astype(_BF16)
    zcols = jnp.zeros((depth, D_MODEL, LANES), _BF16)
    w_lat = jnp.concatenate(
        [w_in[..., :o_kpe], zcols[..., :ROPE_LO], w_in[..., o_kpe:o_kpe + QK_ROPE],
         zcols[..., :LANES - ROPE_HI]], axis=-1)
    w_conv = w_in[..., o_kpe + QK_ROPE:]
    w_uq_p = jnp.pad(w_uq.astype(_BF16).reshape(depth, Q_LORA, N_HEADS, QK_HEAD),
                     ((0, 0), (0, 0), (0, 0), (0, HEAD_PAD - QK_HEAD))).reshape(depth, Q_LORA, QK_WIDTH)
    w_ukv_h = w_ukv.astype(_BF16).reshape(depth, KV_LORA, N_HEADS, QK_NOPE + V_HEAD)
    w_uk_p = jnp.pad(w_ukv_h[..., :QK_NOPE],
                     ((0, 0), (0, 0), (0, 0), (0, HEAD_PAD - QK_NOPE))).reshape(depth, KV_LORA, QK_WIDTH)
    w_uvt = jnp.swapaxes(w_ukv_h[..., QK_NOPE:].reshape(depth, KV_LORA, ATTN_WIDTH), 1, 2)
    q_scale = QK_HEAD ** -0.5 * LOG2_E
    g_qn = jnp.concatenate([g_qn_nope, g_qn_rope, zeros(LANES - ROPE_HI)], axis=-1) * q_scale
    g_qn_t = jnp.broadcast_to(g_qn[:, :, None], (depth, HEAD_PAD, ROW_TILE)).astype(_F32)
    return {
        "g_mix": gain(g_mix), "w_lat": w_lat, "w_conv": w_conv,
        "g_q_lat": gain(g_q_lat), "w_uqt": jnp.swapaxes(w_uq_p, 1, 2),
        "g_kv_lat": gain(g_kv_lat), "w_uk": w_uk_p, "w_uvt": w_uvt,
        "g_qn_t": g_qn_t,
        "g_kn": gain(jnp.concatenate([g_kn_nope, zeros(LANES - QK_NOPE)], axis=-1)),
        "g_kpe": gain(jnp.concatenate([zeros(ROPE_LO), g_kn_rope, zeros(LANES - ROPE_HI)], axis=-1)),
        "conv_w": conv_w.astype(_F32), "g_out_conv": gain(g_out_conv),
        "g_out_attn": gain(g_out_attn), "w_o": w_o.astype(_BF16), "g_mlp": gain(g_mlp),
        "w_up": w_up.astype(_BF16), "w_down": w_down.astype(_BF16), "g_ple": gain(g_ple),
        "w_ple_gate": w_ple_gate.astype(_BF16), "w_ple": w_ple.astype(_BF16),
    }


def kernel(x, p, positions, g_mix, w_in, g_q_lat, w_uq, g_kv_lat, w_ukv, g_qn_nope, g_qn_rope,
           g_kn_nope, g_kn_rope, conv_w, g_out_attn, g_out_conv, w_o, g_mlp, w_up, w_down, g_ple,
           w_ple_gate, w_ple):
    b, s, d = x.shape
    depth = w_in.shape[0]
    assert d == D_MODEL and s % ROW_TILE == 0 and s % Q_TILE == 0
    w = _prepare_weights(g_mix, w_in, g_q_lat, w_uq, g_kv_lat, w_ukv, g_qn_nope, g_qn_rope,
                         g_kn_nope, g_kn_rope, conv_w, g_out_attn, g_out_conv, w_o, g_mlp, w_up,
                         w_down, g_ple, w_ple_gate, w_ple)
    tables = _rope_tables(positions)
    p_rows = p.reshape(depth, b * s, PLE_DIM)
    for layer in range(depth):
        qt, k, vt, conv = _proj_call(layer, x, tables, w)
        attn = _attn_call(qt, k, vt)
        x = _post_call(layer, x.reshape(b * s, d), attn.reshape(b * s, ATTN_WIDTH),
                       conv.reshape(b * s, CONV_WIDTH), p_rows, w).reshape(b, s, d)
    return x
```

```python
import functools

import jax
import jax.numpy as jnp
from jax import lax
from jax.experimental import pallas as pl
from jax.experimental.pallas import tpu as pltpu

D_MODEL = 1024
N_HEADS = 8
QK_NOPE = 64
QK_ROPE = 32
QK_HEAD = QK_NOPE + QK_ROPE
V_HEAD = 64
Q_LORA = 384
KV_LORA = 256
ATTN_WIDTH = N_HEADS * V_HEAD
CONV_WIDTH = D_MODEL - ATTN_WIDTH
CONV_TAPS = 3
D_FF = 4 * D_MODEL
PLE_DIM = 256
ROPE_THETA = 10000.0
EPS = 1e-6
LOG2_E = 1.4426950408889634

LANES = 128
SUBLANES = 8
HEAD_PAD = LANES
QK_WIDTH = N_HEADS * HEAD_PAD
ROPE_LO = QK_NOPE
ROPE_MID = QK_NOPE + QK_ROPE // 2
ROPE_HI = QK_NOPE + QK_ROPE
C_Q = 0
C_KV = C_Q + Q_LORA
C_KPE = C_KV + KV_LORA
LAT_COLS = C_KPE + LANES
C_B = 0
C_C = C_B + CONV_WIDTH
C_X = C_C + CONV_WIDTH
CONV_COLS = C_X + CONV_WIDTH

ROW_TILE = 512
Q_TILE = 512
KV_TILE = 512
ATTN_HEADS_PER_STEP = 4
FF_CHUNK = 1024
POST_TILE = 512
POST_PARTS = 2
VMEM_LIMIT = 56 * 1024 * 1024

_BF16 = jnp.bfloat16
_F32 = jnp.float32


def _dot(a, b):
    return jnp.dot(a, b, preferred_element_type=_F32)


def _rms(x, g, n):
    ms = jnp.sum(x * x, axis=-1, keepdims=True) * (1.0 / n)
    return x * lax.rsqrt(ms + EPS) * g


def _rope_partner(t, lane):
    return jnp.where(lane < ROPE_MID, pltpu.roll(t, LANES - QK_ROPE // 2, axis=1),
                     pltpu.roll(t, QK_ROPE // 2, axis=1))


def _rope_table_kernel(pos_ref, invf_t_ref, cos_ref, sin_ref, cos_t_ref, sin_t_ref):
    c = jnp.cos(pos_ref[0].astype(_F32) * invf_t_ref[...])
    s = jnp.sin(pos_ref[0].astype(_F32) * invf_t_ref[...])
    cos_t_ref[0] = c
    sin_t_ref[0] = s
    t = c.shape[1]
    lo, hi = (ROPE_LO, t), (LANES - ROPE_HI, t)
    cos_ref[0] = jnp.concatenate([jnp.ones(lo, _F32), c, c, jnp.ones(hi, _F32)], axis=0).T
    sin_ref[0] = jnp.concatenate([jnp.zeros(lo, _F32), -s, s, jnp.zeros(hi, _F32)], axis=0).T


def _rope_tables(positions):
    b, s = positions.shape
    half = QK_ROPE // 2
    inv_freq = 1.0 / (ROPE_THETA ** (jnp.arange(0, QK_ROPE, 2, dtype=_F32) / QK_ROPE))
    t = ROW_TILE
    invf_t = jnp.broadcast_to(inv_freq.reshape(half, 1), (half, t))
    out = pl.BlockSpec((1, t, LANES), lambda i, j: (i, j, 0))
    out_t = pl.BlockSpec((1, half, t), lambda i, j: (i, 0, j))
    return pl.pallas_call(
        _rope_table_kernel,
        out_shape=(jax.ShapeDtypeStruct((b, s, LANES), _F32),) * 2
        + (jax.ShapeDtypeStruct((b, half, s), _F32),) * 2,
        grid=(b, s // t),
        in_specs=[pl.BlockSpec((1, 1, t), lambda i, j: (i, 0, j)),
                  pl.BlockSpec((half, t), lambda i, j: (0, 0))],
        out_specs=(out, out, out_t, out_t),
        name="rope_tables",
    )(positions.reshape(b, 1, s), invf_t)


def _proj_kernel(x_ref, cos_ref, sin_ref, cos_t_ref, sin_t_ref, gmix_ref, wlat_ref, wconv_ref, gq_ref,
                 wuqt_ref, gkv_ref, wuk_ref, wuvt_ref, gqn_t_ref, gkn_ref, gkpe_ref, convw_ref,
                 gconv_ref, qt_out, k_out, vt_out, conv_out, u_ref):
    t = x_ref.shape[1]
    lane = lax.broadcasted_iota(jnp.int32, (1, LANES), 1)
    cos = cos_ref[0]
    sin = sin_ref[0]
    nt = (((1,), (1,)), ((), ()))

    @pl.when(pl.program_id(1) == 0)
    def _():
        u_ref[0:SUBLANES, :] = jnp.zeros((SUBLANES, CONV_WIDTH), _F32)

    h = _rms(x_ref[0], gmix_ref[...], D_MODEL).astype(_BF16)
    lat = _dot(h, wlat_ref[...])
    q_lat = lat[:, C_Q:C_KV]
    kv_lat = lat[:, C_KV:C_KPE]
    kpe = lat[:, C_KPE:LAT_COLS]
    u_c = _dot(h, wconv_ref[:, C_C:C_X])

    qn = _rms(q_lat, gq_ref[...], Q_LORA).astype(_BF16)
    kvn = _rms(kv_lat, gkv_ref[...], KV_LORA).astype(_BF16)
    ss_pe = jnp.sum(kpe * kpe, axis=-1, keepdims=True)
    kpe_n = kpe * lax.rsqrt(ss_pe * (1.0 / QK_ROPE) + EPS) * gkpe_ref[...]
    kpe_r = kpe_n * cos + _rope_partner(kpe_n, lane) * sin
    qt = lax.dot_general(wuqt_ref[...], qn, nt, preferred_element_type=_F32)
    kf = _dot(kvn, wuk_ref[...])

    cos_t = cos_t_ref[0]
    sin_t = sin_t_ref[0]
    half = QK_ROPE // 2
    for hd in range(N_HEADS):
        r0 = hd * HEAD_PAD
        nope = qt[r0:r0 + ROPE_LO, :]
        rope = qt[r0 + ROPE_LO:r0 + ROPE_HI, :]
        ss_n = jnp.sum(nope * nope, axis=0, keepdims=True)
        ss_r = jnp.sum(rope * rope, axis=0, keepdims=True)
        nope = nope * lax.rsqrt(ss_n * (1.0 / QK_NOPE) + EPS) * gqn_t_ref[0:ROPE_LO, :]
        rope = rope * lax.rsqrt(ss_r * (1.0 / QK_ROPE) + EPS) * gqn_t_ref[ROPE_LO:ROPE_HI, :]
        x1 = rope[0:half, :]
        x2 = rope[half:QK_ROPE, :]
        qt_out[0, r0:r0 + ROPE_LO, :] = nope.astype(_BF16)
        qt_out[0, r0 + ROPE_LO:r0 + ROPE_MID, :] = (x1 * cos_t - x2 * sin_t).astype(_BF16)
        qt_out[0, r0 + ROPE_MID:r0 + ROPE_HI, :] = (x2 * cos_t + x1 * sin_t).astype(_BF16)
        qt_out[0, r0 + ROPE_HI:r0 + HEAD_PAD, :] = jnp.zeros((HEAD_PAD - ROPE_HI, t), _BF16)

    u_x = _dot(h, wconv_ref[:, C_X:CONV_COLS])
    u = u_c * u_x
    u_ref[SUBLANES:SUBLANES + t, :] = u
    gate_b = _dot(h, wconv_ref[:, C_B:C_C])
    vt_out[0, 0] = lax.dot_general(wuvt_ref[...], kvn, nt,
                                   preferred_element_type=_F32).astype(_BF16)

    rows = t // N_HEADS
    for hd in range(N_HEADS):
        th = kf[:, hd * HEAD_PAD:(hd + 1) * HEAD_PAD]
        ss = jnp.sum(th * th, axis=-1, keepdims=True)
        kh = th * lax.rsqrt(ss * (1.0 / QK_NOPE) + EPS) * gkn_ref[...] + kpe_r
        k_out[0, :, hd * HEAD_PAD:(hd + 1) * HEAD_PAD] = kh.astype(_BF16)

        a = hd * rows
        y = convw_ref[0:1, :] * u[a:a + rows, :]
        for j in range(1, CONV_TAPS):
            y = y + convw_ref[j:j + 1, :] * u_ref[SUBLANES - j + a:SUBLANES - j + a + rows, :]
        conv_out[0, a:a + rows, :] = _rms(gate_b[a:a + rows, :] * y, gconv_ref[...],
                                          CONV_WIDTH).astype(_BF16)
    u_ref[0:SUBLANES, :] = u[t - SUBLANES:t, :]


def _proj_call(layer, x, tables, w):
    b, s, _ = x.shape
    t = ROW_TILE
    cos, sin, cos_t, sin_t = tables

    def row(width):
        return pl.BlockSpec((1, t, width), lambda i, j: (i, j, 0))

    def col(height):
        return pl.BlockSpec((1, height, t), lambda i, j: (i, 0, j))

    def per_layer(arr):
        return pl.BlockSpec((None,) + arr.shape[1:], lambda i, j: (layer,) + (0,) * (arr.ndim - 1))

    params = [w["g_mix"], w["w_lat"], w["w_conv"], w["g_q_lat"], w["w_uqt"], w["g_kv_lat"], w["w_uk"],
              w["w_uvt"], w["g_qn_t"], w["g_kn"], w["g_kpe"], w["conv_w"], w["g_out_conv"]]
    assert t == KV_TILE
    return pl.pallas_call(
        _proj_kernel,
        out_shape=(jax.ShapeDtypeStruct((b, QK_WIDTH, s), _BF16),
                   jax.ShapeDtypeStruct((b, s, QK_WIDTH), _BF16),
                   jax.ShapeDtypeStruct((b, s // t, ATTN_WIDTH, t), _BF16),
                   jax.ShapeDtypeStruct((b, s, CONV_WIDTH), _BF16)),
        grid=(b, s // t),
        in_specs=[row(D_MODEL), row(LANES), row(LANES), col(QK_ROPE // 2), col(QK_ROPE // 2)]
                 + [per_layer(a) for a in params],
        out_specs=(col(QK_WIDTH), row(QK_WIDTH),
                   pl.BlockSpec((1, 1, ATTN_WIDTH, t), lambda i, j: (i, j, 0, 0)),
                   row(CONV_WIDTH)),
        scratch_shapes=[pltpu.VMEM((SUBLANES + t, CONV_WIDTH), _F32)],
        compiler_params=pltpu.CompilerParams(
            dimension_semantics=("arbitrary", "arbitrary"), vmem_limit_bytes=VMEM_LIMIT),
        name="proj",
    )(x, cos, sin, cos_t, sin_t, *params)


def _attn_kernel(qt_ref, qt_next_ref, k_ref, vt_ref, o_ref, m_ref, acc_ref, s0_ref, s1_ref,
                 xa_ref, xb_ref):
    tq = qt_ref.shape[2]
    tk = KV_TILE
    qi = pl.program_id(2)
    heads = qt_ref.shape[1] // HEAD_PAD

    m_ref[...] = jnp.full(m_ref.shape, -jnp.inf, _F32)
    acc_ref[...] = jnp.zeros(acc_ref.shape, _F32)
    ones = jnp.ones((V_HEAD, tk), _BF16)

    def load_values(t):
        return [vt_ref[0, t, hh * V_HEAD:(hh + 1) * V_HEAD, :] for hh in range(heads)]

    def load_keys(t):
        start = t * tk if isinstance(t, int) else pl.multiple_of(t * tk, tk)
        return [k_ref[0, pl.ds(start, tk), hh * HEAD_PAD:(hh + 1) * HEAD_PAD]
                for hh in range(heads)]

    def scores(hh, kbs, s_ref, q_ref):
        qt = q_ref[0, hh * HEAD_PAD:(hh + 1) * HEAD_PAD, :]
        st = _dot(kbs[hh], qt)
        s_ref[hh, 0:tk, :] = st
        s_ref[hh, tk:tk + 1, :] = jnp.max(st, axis=0, keepdims=True)

    def accumulate(hh, st, block_max, vt, q0, q1):
        m_old = m_ref[hh, :, q0:q1]
        m_new = jnp.maximum(m_old, block_max)
        alpha = jnp.exp2(m_old - m_new)
        pt = jnp.exp2(st - m_new).astype(_BF16)
        acc_ref[hh, :, q0:q1] = alpha * acc_ref[hh, :, q0:q1] + _dot(vt, pt)
        m_ref[hh, :, q0:q1] = m_new

    def causal(st):
        key = lax.broadcasted_iota(jnp.int32, st.shape, 0)
        qry = lax.broadcasted_iota(jnp.int32, st.shape, 1)
        return jnp.where(key <= qry, st, -jnp.inf)

    def softmax_values(hh, s_ref, vts, masked):
        vt = jnp.concatenate([vts[hh], ones], axis=0)
        if not masked:
            accumulate(hh, s_ref[hh, 0:tk, :], s_ref[hh, tk:tk + 1, :], vt, 0, tq)
            return
        hk = tk // 2
        st = causal(s_ref[hh, 0:hk, :])
        accumulate(hh, st, jnp.max(st, axis=0, keepdims=True), vt[:, 0:hk], 0, tq)
        st = causal(s_ref[hh, hk:tk, hk:tq])
        accumulate(hh, st, jnp.max(st, axis=0, keepdims=True), vt[:, hk:tk], hk, tq)

    def score_stage(kbs, s_ref, q_ref=qt_ref):
        for hh in range(heads):
            scores(hh, kbs, s_ref, q_ref)

    def fused_step(t_cur, s_cur, masked, t_next, s_next, q_next):
        vts = load_values(t_cur)
        kbs = load_keys(t_next)
        for hh in range(heads):
            scores(hh, kbs, s_next, q_next)
            softmax_values(hh, s_cur, vts, masked)

    def pipelined_step(t, s_cur, s_next):
        fused_step(t, s_cur, False, t + 1, s_next, qt_ref)

    def last_step(t, s_cur, x_out):
        fused_step(t, s_cur, True, 0, x_out, qt_next_ref)
        outs = []
        for hh in range(heads):
            acc = acc_ref[hh]
            outs.append(acc[0:V_HEAD, :] / acc[V_HEAD:V_HEAD + 1, :])
        o_ref[0] = jnp.concatenate(outs, axis=0).T.astype(o_ref.dtype)

    def middle_steps(x_in):
        pipelined_step(0, x_in, s0_ref)

        def pair(p, carry):
            pipelined_step(2 * p + 1, s0_ref, s1_ref)
            pipelined_step(2 * p + 2, s1_ref, s0_ref)
            return carry

        lax.fori_loop(0, lax.shift_right_logical(qi - 1, 1), pair, 0)

    odd = (qi & 1) == 1

    @pl.when(qi == 0)
    def _():
        score_stage(load_keys(0), xa_ref)
        last_step(0, xa_ref, xb_ref)

    @pl.when(jnp.logical_and(qi > 0, jnp.logical_not(odd)))
    def _():
        middle_steps(xa_ref)
        pipelined_step(qi - 1, s0_ref, s1_ref)
        last_step(qi, s1_ref, xb_ref)

    @pl.when(odd)
    def _():
        middle_steps(xb_ref)
        last_step(qi, s0_ref, xa_ref)


def _attn_call(qt, k, vt):
    b, s, _ = k.shape
    heads = ATTN_HEADS_PER_STEP
    assert (heads * V_HEAD) % LANES == 0 and N_HEADS % heads == 0
    groups = N_HEADS // heads
    tq = Q_TILE
    assert Q_TILE == KV_TILE
    return pl.pallas_call(
        _attn_kernel,
        out_shape=jax.ShapeDtypeStruct((b, s, ATTN_WIDTH), _BF16),
        grid=(b, groups, s // tq),
        in_specs=[pl.BlockSpec((1, heads * HEAD_PAD, tq), lambda i, g, j: (i, g, j)),
                  pl.BlockSpec((1, heads * HEAD_PAD, tq),
                               lambda i, g, j: (i, g, jnp.minimum(j + 1, s // tq - 1))),
                  pl.BlockSpec((1, s, heads * HEAD_PAD), lambda i, g, j: (i, 0, g)),
                  pl.BlockSpec((1, s // KV_TILE, heads * V_HEAD, KV_TILE),
                               lambda i, g, j: (i, 0, g, 0))],
        out_specs=pl.BlockSpec((1, tq, heads * V_HEAD), lambda i, g, j: (i, j, g)),
        scratch_shapes=[pltpu.VMEM((heads, 1, tq), _F32),
                        pltpu.VMEM((heads, 2 * V_HEAD, tq), _F32),
                        ] + [pltpu.VMEM((heads, KV_TILE + SUBLANES, tq), _F32)] * 4,
        compiler_params=pltpu.CompilerParams(
            dimension_semantics=("arbitrary", "arbitrary", "arbitrary"),
            vmem_limit_bytes=VMEM_LIMIT),
        name="attn",
    )(qt, qt, k, vt)


def _post_kernel(x_ref, attn_ref, conv_ref, p_ref, gattn_ref, wo_ref, gmlp_ref, wup_ref,
                 wdown_ref, gple_ref, wgate_ref, wple_ref, o_ref):
    t = x_ref.shape[0]
    rows = t // POST_PARTS
    halves = [slice(i * rows, (i + 1) * rows) for i in range(POST_PARTS)]

    def out_proj(r):
        an = _rms(attn_ref[r, :].astype(_F32), gattn_ref[...], ATTN_WIDTH).astype(_BF16)
        return (x_ref[r, :] + _dot(an, wo_ref[0:ATTN_WIDTH, :])
                + _dot(conv_ref[r, :], wo_ref[ATTN_WIDTH:, :]))

    def mlp(x):
        h2 = _rms(x, gmlp_ref[...], D_MODEL).astype(_BF16)
        acc = None
        for c in range(D_FF // FF_CHUNK):
            up = _dot(h2, wup_ref[:, c * FF_CHUNK:(c + 1) * FF_CHUNK])
            act = jnp.square(jnp.maximum(up, 0.0)).astype(_BF16)
            part = _dot(act, wdown_ref[c * FF_CHUNK:(c + 1) * FF_CHUNK, :])
            acc = part if acc is None else acc + part
        return x + acc

    def embed(r):
        return _dot(p_ref[r, :].astype(_BF16), wple_ref[...])

    def gated_add(r, x, emb):
        gate = jax.nn.sigmoid(_dot(_rms(x, gple_ref[...], D_MODEL).astype(_BF16), wgate_ref[...]))
        o_ref[r, :] = x + gate * emb

    embs, xs = [], []
    for r in halves:
        embs.append(embed(r))
        xs.append(out_proj(r))
    xs = [mlp(x) for x in xs]
    for r, x, emb in zip(halves, xs, embs):
        gated_add(r, x, emb)


def _post_call(layer, x, attn, conv, p, w):
    n = x.shape[0]
    t = POST_TILE

    def row(width):
        return pl.BlockSpec((t, width), lambda i: (i, 0))

    def per_layer(arr):
        return pl.BlockSpec((None,) + arr.shape[1:], lambda i: (layer,) + (0,) * (arr.ndim - 1),
                            pipeline_mode=pl.Buffered(1))

    params = [w["g_out_attn"], w["w_o"], w["g_mlp"], w["w_up"], w["w_down"], w["g_ple"],
              w["w_ple_gate"], w["w_ple"]]
    return pl.pallas_call(
        _post_kernel,
        out_shape=jax.ShapeDtypeStruct((n, D_MODEL), _F32),
        grid=(n // t,),
        in_specs=[row(D_MODEL), row(ATTN_WIDTH), row(CONV_WIDTH),
                  pl.BlockSpec((None, t, PLE_DIM), lambda i: (layer, i, 0))]
                 + [per_layer(a) for a in params],
        out_specs=row(D_MODEL),
        compiler_params=pltpu.CompilerParams(
            dimension_semantics=("arbitrary",), vmem_limit_bytes=VMEM_LIMIT),
        name="post",
    )(x, attn, conv, p, *params)


def _prepare_weights(g_mix, w_in, g_q_lat, w_uq, g_kv_lat, w_ukv, g_qn_nope, g_qn_rope,
                     g_kn_nope, g_kn_rope, conv_w, g_out_attn, g_out_conv, w_o, g_mlp, w_up,
                     w_down, g_ple, w_ple_gate, w_ple):
    depth = w_in.shape[0]

    def gain(g):
        return g.reshape(depth, 1, -1).astype(_F32)

    def zeros(*shape):
        return jnp.zeros((depth,) + shape, _F32)

    o_kpe = Q_LORA + KV_LORA
    w_in = w_in.astype(_BF16)
    zcols = jnp.zeros((depth, D_MODEL, LANES), _BF16)
    w_lat = jnp.concatenate(
        [w_in[..., :o_kpe], zcols[..., :ROPE_LO], w_in[..., o_kpe:o_kpe + QK_ROPE],
         zcols[..., :LANES - ROPE_HI]], axis=-1)
    w_conv = w_in[..., o_kpe + QK_ROPE:]
    w_uq_p = jnp.pad(w_uq.astype(_BF16).reshape(depth, Q_LORA, N_HEADS, QK_HEAD),
                     ((0, 0), (0, 0), (0, 0), (0, HEAD_PAD - QK_HEAD))).reshape(depth, Q_LORA, QK_WIDTH)
    w_ukv_h = w_ukv.astype(_BF16).reshape(depth, KV_LORA, N_HEADS, QK_NOPE + V_HEAD)
    w_uk_p = jnp.pad(w_ukv_h[..., :QK_NOPE],
                     ((0, 0), (0, 0), (0, 0), (0, HEAD_PAD - QK_NOPE))).reshape(depth, KV_LORA, QK_WIDTH)
    w_uvt = jnp.swapaxes(w_ukv_h[..., QK_NOPE:].reshape(depth, KV_LORA, ATTN_WIDTH), 1, 2)
    q_scale = QK_HEAD ** -0.5 * LOG2_E
    g_qn = jnp.concatenate([g_qn_nope, g_qn_rope, zeros(LANES - ROPE_HI)], axis=-1) * q_scale
    g_qn_t = jnp.broadcast_to(g_qn[:, :, None], (depth, HEAD_PAD, ROW_TILE)).astype(_F32)
    return {
        "g_mix": gain(g_mix), "w_lat": w_lat, "w_conv": w_conv,
        "g_q_lat": gain(g_q_lat), "w_uqt": jnp.swapaxes(w_uq_p, 1, 2),
        "g_kv_lat": gain(g_kv_lat), "w_uk": w_uk_p, "w_uvt": w_uvt,
        "g_qn_t": g_qn_t,
        "g_kn": gain(jnp.concatenate([g_kn_nope, zeros(LANES - QK_NOPE)], axis=-1)),
        "g_kpe": gain(jnp.concatenate([zeros(ROPE_LO), g_kn_rope, zeros(LANES - ROPE_HI)], axis=-1)),
        "conv_w": conv_w.astype(_F32), "g_out_conv": gain(g_out_conv),
        "g_out_attn": gain(g_out_attn), "w_o": w_o.astype(_BF16), "g_mlp": gain(g_mlp),
        "w_up": w_up.astype(_BF16), "w_down": w_down.astype(_BF16), "g_ple": gain(g_ple),
        "w_ple_gate": w_ple_gate.astype(_BF16), "w_ple": w_ple.astype(_BF16),
    }


def kernel(x, p, positions, g_mix, w_in, g_q_lat, w_uq, g_kv_lat, w_ukv, g_qn_nope, g_qn_rope,
           g_kn_nope, g_kn_rope, conv_w, g_out_attn, g_out_conv, w_o, g_mlp, w_up, w_down, g_ple,
           w_ple_gate, w_ple):
    b, s, d = x.shape
    depth = w_in.shape[0]
    assert d == D_MODEL and s % ROW_TILE == 0 and s % Q_TILE == 0
    w = _prepare_weights(g_mix, w_in, g_q_lat, w_uq, g_kv_lat, w_ukv, g_qn_nope, g_qn_rope,
                         g_kn_nope, g_kn_rope, conv_w, g_out_attn, g_out_conv, w_o, g_mlp, w_up,
                         w_down, g_ple, w_ple_gate, w_ple)
    tables = _rope_tables(positions)
    p_rows = p.reshape(depth, b * s, PLE_DIM)
    for layer in range(depth):
        qt, k, vt, conv = _proj_call(layer, x, tables, w)
        attn = _attn_call(qt, k, vt)
        x = _post_call(layer, x.reshape(b * s, d), attn.reshape(b * s, ATTN_WIDTH),
                       conv.reshape(b * s, CONV_WIDTH), p_rows, w).reshape(b, s, d)
    return x
```
